```python
import jax, jax.numpy as jnp
from jax import lax
import numpy as np

D_MODEL = 1024
BATCH = 2
SEQ = 16384
DEPTH = 4

CHUNK = 64
N_A = DEPTH // 2
N_B = DEPTH - N_A
RET_HEADS = 4
RET_QK_DIM = D_MODEL // RET_HEADS
RET_V_DIM = 2 * D_MODEL // RET_HEADS
ROPE_BASE = 10000.0
FOX_HEADS = 16
FOX_HEAD_DIM = D_MODEL // FOX_HEADS
Q_BLOCK = 128
FORGET_BIAS = 3.0
D_FF = 2816
N_EXPERTS = 8
TOP_K = 2
D_FF_EXPERT = 3584
N_DENSE = (DEPTH + 1) // 2
N_MOE = DEPTH // 2
EPS = 1e-6

kernel_name = "yoco_retention_fox_moe_trunk"


def rmsnorm(x, g):
    xf = x.astype(jnp.float32)
    y = xf * lax.rsqrt(jnp.mean(xf * xf, axis=-1, keepdims=True) + EPS)
    return (y * g.astype(jnp.float32)).astype(x.dtype)


def modulate(h, shift, scale):
    return h * (1.0 + scale[:, None, :]) + shift[:, None, :]


def rotary(x, pos):
    d = x.shape[-1]
    inv = 1.0 / (ROPE_BASE ** (jnp.arange(0, d, 2, dtype=jnp.float32) / d))
    ang = pos.astype(jnp.float32)[:, None] * inv[None, :]
    cos = jnp.cos(ang)[None, :, None, :]
    sin = jnp.sin(ang)[None, :, None, :]
    x1 = x[..., : d // 2].astype(jnp.float32)
    x2 = x[..., d // 2:].astype(jnp.float32)
    return jnp.concatenate([x1 * cos - x2 * sin, x2 * cos + x1 * sin], axis=-1).astype(x.dtype)


def retention(h, w_in, w_o):
    B, S, D = h.shape
    nc = S // CHUNK
    proj = h @ w_in
    q, k, v, g = jnp.split(proj, [D, 2 * D, 4 * D], axis=-1)
    pos = jnp.arange(S)
    q = rotary(q.reshape(B, S, RET_HEADS, RET_QK_DIM), pos)
    k = rotary(k.reshape(B, S, RET_HEADS, RET_QK_DIM), pos) * (RET_QK_DIM ** -0.5)
    v = v.reshape(B, S, RET_HEADS, RET_V_DIM)

    def to_chunks(t):
        return t.reshape(B, nc, CHUNK, RET_HEADS, -1).transpose(1, 0, 3, 2, 4).astype(jnp.float32)

    log_g = jnp.log(1.0 - 2.0 ** (-5.0 - jnp.arange(RET_HEADS, dtype=jnp.float32)))
    idx = jnp.arange(CHUNK, dtype=jnp.float32)
    intra_decay = jnp.exp(log_g[:, None, None] * jnp.abs(idx[:, None] - idx[None, :]))
    q_decay = jnp.exp(log_g[:, None] * (idx[None, :] + 1.0))[None, :, :, None]
    k_decay = jnp.exp(log_g[:, None] * (CHUNK - 1.0 - idx[None, :]))[None, :, :, None]
    chunk_decay = jnp.exp(log_g * CHUNK)[None, :, None, None]

    def step(state, qkv):
        qc, kc, vc = qkv
        scores = jnp.einsum('bhnd,bhmd->bhnm', qc, kc) * intra_decay
        o = (jnp.einsum('bhnm,bhmv->bhnv', scores, vc)
             + jnp.einsum('bhnd,bhdv->bhnv', qc, state) * q_decay)
        state = state * chunk_decay + jnp.einsum('bhmd,bhmv->bhdv', kc * k_decay, vc)
        return state, o

    state0 = jnp.zeros((B, RET_HEADS, RET_QK_DIM, RET_V_DIM), jnp.float32)
    _, o = lax.scan(step, state0, (to_chunks(q), to_chunks(k), to_chunks(v)))
    o = o.transpose(1, 0, 3, 2, 4).reshape(B, S, RET_HEADS, RET_V_DIM)
    mu = jnp.mean(o, axis=-1, keepdims=True)
    var = jnp.mean(jnp.square(o - mu), axis=-1, keepdims=True)
    y = ((o - mu) * lax.rsqrt(var + EPS)).reshape(B, S, 2 * D).astype(h.dtype)
    return (jax.nn.silu(g) * y) @ w_o


def fox_shared_kv(hn, w_kv, w_f, b_f):
    B, S, D = hn.shape
    k, v = jnp.split(hn @ w_kv, 2, axis=-1)
    k = k.reshape(B, S, FOX_HEADS, FOX_HEAD_DIM).transpose(0, 2, 1, 3)
    v = v.reshape(B, S, FOX_HEADS, FOX_HEAD_DIM).transpose(0, 2, 1, 3)
    log_f = jax.nn.log_sigmoid((hn @ w_f).astype(jnp.float32) + b_f.astype(jnp.float32))
    F = jnp.cumsum(log_f, axis=1).transpose(0, 2, 1)
    return k, v, F


def forgetting_attention(hn, w_qg, w_o, k, v, F):
    B, S, D = hn.shape
    nb = S // Q_BLOCK
    q, g = jnp.split(hn @ w_qg, 2, axis=-1)
    qb_all = q.reshape(B, nb, Q_BLOCK, FOX_HEADS, FOX_HEAD_DIM).transpose(1, 0, 3, 2, 4)
    Fq_all = F.reshape(B, FOX_HEADS, nb, Q_BLOCK).transpose(2, 0, 1, 3)
    key_pos = jnp.arange(S)
    scale = FOX_HEAD_DIM ** -0.5

    def block(args):
        qb, Fqb, i = args
        q_pos = i * Q_BLOCK + jnp.arange(Q_BLOCK)
        s = (jnp.einsum('bhqd,bhkd->bhqk', qb, k).astype(jnp.float32) * scale
             + (Fqb[..., :, None] - F[:, :, None, :]))
        s = jnp.where(key_pos[None, :] <= q_pos[:, None], s, -jnp.inf)
        p = jax.nn.softmax(s, axis=-1)
        return jnp.einsum('bhqk,bhkd->bhqd', p.astype(v.dtype), v)

    o = lax.map(block, (qb_all, Fq_all, jnp.arange(nb)))
    o = o.transpose(1, 0, 3, 2, 4).reshape(B, S, D)
    return (o * jax.nn.sigmoid(g)) @ w_o


def swiglu(h, w_gate, w_up, w_down):
    return (jax.nn.silu(h @ w_gate) * (h @ w_up)) @ w_down


def moe_swiglu(h, w_router, b_router, w_gate, w_up, w_down):
    logits = (h @ w_router).astype(jnp.float32) + b_router.astype(jnp.float32)
    top_val, top_idx = lax.top_k(logits, TOP_K)
    top_w = jax.nn.softmax(top_val, axis=-1)
    gates = jnp.sum(jax.nn.one_hot(top_idx, N_EXPERTS, dtype=jnp.float32) * top_w[..., None], axis=-2)
    gates = gates.astype(h.dtype)
    out = jnp.zeros_like(h)
    for e in range(N_EXPERTS):
        out = out + gates[..., e:e + 1] * swiglu(h, w_gate[e], w_up[e], w_down[e])
    return out


def setup_inputs(seed: int = 0) -> dict:
    key = jax.random.key(seed)
    ks = jax.random.split(key, 32)
    D = D_MODEL
    f32 = jnp.float32

    def nrm(k, shape):
        return jax.random.normal(k, shape, f32)

    def w(k, shape, fan_in, scale=1.0):
        return nrm(k, shape) * (scale * fan_in ** -0.5)

    return {
        "x": nrm(ks[0], (BATCH, SEQ, D)),
        "c": nrm(ks[1], (BATCH, D)),
        "ada_w": w(ks[2], (DEPTH, D, 6 * D), D, 0.5),
        "ada_b": 0.02 * nrm(ks[3], (DEPTH, 6 * D)),
        "norm_g": 1.0 + 0.01 * nrm(ks[4], (DEPTH, 2, D)),
        "ret_w_in": w(ks[5], (N_A, D, 6 * D), D),
        "ret_w_o": w(ks[6], (N_A, 2 * D, D), 2 * D),
        "kv_ada_w": w(ks[7], (D, 2 * D), D, 0.5),
        "kv_ada_b": 0.02 * nrm(ks[8], (2 * D,)),
        "kv_norm_g": 1.0 + 0.01 * nrm(ks[9], (D,)),
        "fox_w_kv": w(ks[10], (D, 2 * D), D),
        "fox_w_f": w(ks[11], (D, FOX_HEADS), D),
        "fox_b_f": FORGET_BIAS + 0.1 * nrm(ks[12], (FOX_HEADS,)),
        "fox_w_qg": w(ks[13], (N_B, D, 2 * D), D),
        "fox_w_o": w(ks[14], (N_B, D, D), D),
        "ffn_w_gate": w(ks[15], (N_DENSE, D, D_FF), D),
        "ffn_w_up": w(ks[16], (N_DENSE, D, D_FF), D),
        "ffn_w_down": w(ks[17], (N_DENSE, D_FF, D), D_FF),
        "router_w": w(ks[18], (N_MOE, D, N_EXPERTS), D),
        "router_b": 0.01 * nrm(ks[19], (N_MOE, N_EXPERTS)),
        "moe_w_gate": w(ks[20], (N_MOE, N_EXPERTS, D, D_FF_EXPERT), D),
        "moe_w_up": w(ks[21], (N_MOE, N_EXPERTS, D, D_FF_EXPERT), D),
        "moe_w_down": w(ks[22], (N_MOE, N_EXPERTS, D_FF_EXPERT, D), D_FF_EXPERT),
        "final_ada_w": w(ks[23], (D, 2 * D), D, 0.5),
        "final_ada_b": 0.02 * nrm(ks[24], (2 * D,)),
        "final_norm_g": 1.0 + 0.01 * nrm(ks[25], (D,)),
    }


def reference(x, c, ada_w, ada_b, norm_g, ret_w_in, ret_w_o, kv_ada_w, kv_ada_b, kv_norm_g,
              fox_w_kv, fox_w_f, fox_b_f, fox_w_qg, fox_w_o, ffn_w_gate, ffn_w_up, ffn_w_down,
              router_w, router_b, moe_w_gate, moe_w_up, moe_w_down, final_ada_w, final_ada_b,
              final_norm_g):
    c_act = jax.nn.silu(c)
    h = x
    shared = None
    for l in range(DEPTH):
        sh1, sc1, g1, sh2, sc2, g2 = jnp.split(c_act @ ada_w[l] + ada_b[l], 6, axis=-1)
        if l == N_A:
            kv_sh, kv_sc = jnp.split(c_act @ kv_ada_w + kv_ada_b, 2, axis=-1)
            hn_kv = modulate(rmsnorm(h, kv_norm_g), kv_sh, kv_sc)
            shared = fox_shared_kv(hn_kv, fox_w_kv, fox_w_f, fox_b_f)
        a = modulate(rmsnorm(h, norm_g[l, 0]), sh1, sc1)
        if l < N_A:
            mix = retention(a, ret_w_in[l], ret_w_o[l])
        else:
            j = l - N_A
            mix = forgetting_attention(a, fox_w_qg[j], fox_w_o[j], shared[0], shared[1], shared[2])
        h = h + g1[:, None, :] * mix
        m = modulate(rmsnorm(h, norm_g[l, 1]), sh2, sc2)
        if l % 2 == 0:
            i = l // 2
            ff = swiglu(m, ffn_w_gate[i], ffn_w_up[i], ffn_w_down[i])
        else:
            i = l // 2
            ff = moe_swiglu(m, router_w[i], router_b[i], moe_w_gate[i], moe_w_up[i], moe_w_down[i])
        h = h + g2[:, None, :] * ff
    f_sh, f_sc = jnp.split(c_act @ final_ada_w + final_ada_b, 2, axis=-1)
    return modulate(rmsnorm(h, final_norm_g), f_sh, f_sc)
```

```python
import functools
import math

import jax
import jax.numpy as jnp
from jax import lax
from jax.experimental import pallas as pl
from jax.experimental.pallas import tpu as pltpu

F32 = jnp.float32
BF16 = jnp.bfloat16

CHUNK = 64
RET_HEADS = 4
FOX_HEADS = 16
ROPE_BASE = 10000.0
TOP_K = 2
EPS = 1e-6

LANES = 128
VMEM_LIMIT_BYTES = 56 * 1024 * 1024
NEG_BIG = -1e30


def _params(*sem):
    return pltpu.CompilerParams(dimension_semantics=sem, vmem_limit_bytes=VMEM_LIMIT_BYTES)


def _silu(x):
    return x * (1.0 / (1.0 + jnp.exp(-x)))


def _sigmoid(x):
    return 1.0 / (1.0 + jnp.exp(-x))


def _norm_mod(x, g, sh, sc):
    var = jnp.mean(x * x, axis=-1, keepdims=True)
    y = (x * lax.rsqrt(var + EPS)) * g
    return y * (1.0 + sc) + sh


def _ada_kernel(c_ref, w_ref, b_ref, o_ref):
    ca = _silu(c_ref[...]).astype(BF16)
    o_ref[...] = jnp.dot(ca, w_ref[...].astype(BF16), preferred_element_type=F32) + b_ref[...]


def _ada(c_pad, w, b, tn=1024):
    L, D, N = w.shape
    return pl.pallas_call(
        _ada_kernel,
        grid=(L, N // tn),
        in_specs=[
            pl.BlockSpec((8, D), lambda l, j: (0, 0)),
            pl.BlockSpec((None, D, tn), lambda l, j: (l, 0, j)),
            pl.BlockSpec((None, 1, tn), lambda l, j: (l, 0, j)),
        ],
        out_specs=pl.BlockSpec((None, 8, tn), lambda l, j: (l, 0, j)),
        out_shape=jax.ShapeDtypeStruct((L, 8, N), F32),
        compiler_params=_params("parallel", "parallel"),
        name="ada_proj",
    )(c_pad, w, b)


def _nmm_kernel(*refs, n_chunks, cw, modes, scales, out_dtype):
    h_ref, g_ref, sh_ref, sc_ref, w_ref = refs[:5]
    has_rot = any(m == "rot" for m in modes)
    if has_rot:
        cos_ref, sin_ref, o_ref = refs[5:8]
    else:
        o_ref = refs[5]
    a = _norm_mod(h_ref[...], g_ref[...], sh_ref[...], sc_ref[...]).astype(BF16)
    for c in range(n_chunks):
        acc = jnp.dot(a, w_ref[:, c * cw:(c + 1) * cw], preferred_element_type=F32)
        if modes[c] == "rot":
            cos = cos_ref[...]
            sin = sin_ref[...]
            parts = []
            for hh in range(cw // (2 * LANES)):
                x1 = acc[:, (2 * hh) * LANES:(2 * hh + 1) * LANES]
                x2 = acc[:, (2 * hh + 1) * LANES:(2 * hh + 2) * LANES]
                parts.append(x1 * cos - x2 * sin)
                parts.append(x2 * cos + x1 * sin)
            acc = jnp.concatenate(parts, axis=-1)
        if scales[c] != 1.0:
            acc = acc * scales[c]
        o_ref[:, c * cw:(c + 1) * cw] = acc.astype(out_dtype)


def _norm_mod_matmul(h, g, sh, sc, w, S, *, modes, scales, cw, out_dtype, tm, rot=None):
    T, D = h.shape
    N = w.shape[1]
    n_chunks = N // cw
    per_b = S // tm
    in_specs = [
        pl.BlockSpec((tm, D), lambda i: (i, 0)),
        pl.BlockSpec((1, D), lambda i: (0, 0)),
        pl.BlockSpec((None, 1, D), lambda i: (i // per_b, 0, 0)),
        pl.BlockSpec((None, 1, D), lambda i: (i // per_b, 0, 0)),
        pl.BlockSpec((D, N), lambda i: (0, 0)),
    ]
    args = [h, g, sh, sc, w]
    if rot is not None:
        in_specs += [pl.BlockSpec((tm, LANES), lambda i: (i % per_b, 0))] * 2
        args += list(rot)
    return pl.pallas_call(
        functools.partial(_nmm_kernel, n_chunks=n_chunks, cw=cw, modes=modes, scales=scales,
                          out_dtype=out_dtype),
        grid=(T // tm,),
        in_specs=in_specs,
        out_specs=pl.BlockSpec((tm, N), lambda i: (i, 0)),
        out_shape=jax.ShapeDtypeStruct((T, N), out_dtype),
        compiler_params=_params("parallel"),
        name="norm_mod_matmul",
    )(*args)


def _ret_kernel(q_ref, k_ref, v_ref, g_ref, intra_ref, qd_ref, kd_ref, cd_ref, o_ref, st_ref, *,
                n_sub):
    @pl.when(pl.program_id(2) == 0)
    def _():
        st_ref[...] = jnp.zeros_like(st_ref)

    intra = intra_ref[...]
    qd = qd_ref[...]
    kd = kd_ref[...]
    cd = cd_ref[...]

    def body(c, carry):
        r0 = pl.multiple_of(c * CHUNK, CHUNK)
        qc = q_ref[pl.ds(r0, CHUNK), :]
        kc = k_ref[pl.ds(r0, CHUNK), :]
        vc = v_ref[pl.ds(r0, CHUNK), :]
        state = st_ref[...]
        scores = lax.dot_general(qc, kc, (((1,), (1,)), ((), ())), preferred_element_type=F32) * intra
        o = (jnp.dot(scores.astype(BF16), vc, preferred_element_type=F32)
             + jnp.dot(qc, state.astype(BF16), preferred_element_type=F32) * qd)
        kdk = (kc.astype(F32) * kd).astype(BF16)
        st_ref[...] = state * cd + lax.dot_general(kdk, vc, (((0,), (0,)), ((), ())),
                                                   preferred_element_type=F32)
        mu = jnp.mean(o, axis=-1, keepdims=True)
        d = o - mu
        var = jnp.mean(d * d, axis=-1, keepdims=True)
        y = d * lax.rsqrt(var + EPS)
        o_ref[pl.ds(r0, CHUNK), :] = (_silu(g_ref[pl.ds(r0, CHUNK), :]) * y).astype(o_ref.dtype)
        return carry

    lax.fori_loop(0, n_sub, body, 0)


def _retention(qkv, gate, tabs, B, S, lc):
    T = qkv.shape[0]
    D = qkv.shape[1] // 4
    dk = D // RET_HEADS
    dv = 2 * D // RET_HEADS
    H = RET_HEADS
    per_b = S // lc
    intra, qd, kd, cd = tabs
    row = lambda b, h, s: b * per_b + s
    return pl.pallas_call(
        functools.partial(_ret_kernel, n_sub=lc // CHUNK),
        grid=(B, H, per_b),
        in_specs=[
            pl.BlockSpec((lc, dk), lambda b, h, s: (row(b, h, s), h)),
            pl.BlockSpec((lc, dk), lambda b, h, s: (row(b, h, s), H + h)),
            pl.BlockSpec((lc, dv), lambda b, h, s: (row(b, h, s), H + h)),
            pl.BlockSpec((lc, dv), lambda b, h, s: (row(b, h, s), h)),
            pl.BlockSpec((None, CHUNK, CHUNK), lambda b, h, s: (h, 0, 0)),
            pl.BlockSpec((None, CHUNK, 1), lambda b, h, s: (h, 0, 0)),
            pl.BlockSpec((None, CHUNK, 1), lambda b, h, s: (h, 0, 0)),
            pl.BlockSpec((None, 1, 1), lambda b, h, s: (h, 0, 0)),
        ],
        out_specs=pl.BlockSpec((lc, dv), lambda b, h, s: (row(b, h, s), h)),
        out_shape=jax.ShapeDtypeStruct((T, 2 * D), BF16),
        scratch_shapes=[pltpu.VMEM((dk, dv), F32)],
        compiler_params=_params("parallel", "parallel", "arbitrary"),
        name="retention",
    )(qkv, qkv, qkv, gate, intra, qd, kd, cd)


def _retention_tables():
    h = jnp.arange(RET_HEADS, dtype=F32)
    log_g = jnp.log(1.0 - 2.0 ** (-5.0 - h))
    idx = jnp.arange(CHUNK, dtype=F32)
    intra = jnp.exp(log_g[:, None, None] * jnp.abs(idx[:, None] - idx[None, :]))
    qd = jnp.exp(log_g[:, None] * (idx[None, :] + 1.0))[:, :, None]
    kd = jnp.exp(log_g[:, None] * (CHUNK - 1.0 - idx[None, :]))[:, :, None]
    cd = jnp.exp(log_g * CHUNK)[:, None, None]
    return intra, qd, kd, cd


def _rotary_tables(S, d):
    inv = 1.0 / (ROPE_BASE ** (jnp.arange(0, d, 2, dtype=F32) / d))
    ang = jnp.arange(S).astype(F32)[:, None] * inv[None, :]
    return jnp.cos(ang), jnp.sin(ang)


def _proj_res_kernel(*refs, gated):
    if gated:
        o_in, og_ref, w_ref, h_ref, gt_ref, out_ref = refs
        x = (o_in[...] * _sigmoid(og_ref[...])).astype(BF16)
    else:
        x_ref, w_ref, h_ref, gt_ref, out_ref = refs
        x = x_ref[...]
    out_ref[...] = h_ref[...] + gt_ref[...] * jnp.dot(x, w_ref[...], preferred_element_type=F32)


def _proj_residual(xs, w, h, gate, S, tm):
    T, D = h.shape
    K = w.shape[0]
    per_b = S // tm
    gated = len(xs) == 2
    in_specs = [pl.BlockSpec((tm, K), lambda i: (i, 0)) for _ in xs] + [
        pl.BlockSpec((K, D), lambda i: (0, 0)),
        pl.BlockSpec((tm, D), lambda i: (i, 0)),
        pl.BlockSpec((None, 1, D), lambda i: (i // per_b, 0, 0)),
    ]
    return pl.pallas_call(
        functools.partial(_proj_res_kernel, gated=gated),
        grid=(T // tm,),
        in_specs=in_specs,
        out_specs=pl.BlockSpec((tm, D), lambda i: (i, 0)),
        out_shape=jax.ShapeDtypeStruct((T, D), F32),
        compiler_params=_params("parallel"),
        name="proj_residual",
    )(*xs, w, h, gate)


def _ffn_dense_kernel(h_ref, g_ref, sh_ref, sc_ref, gt_ref, wg_ref, wu_ref, wd_ref, o_ref, a_scr,
                      acc_scr):
    f = pl.program_id(1)

    @pl.when(f == 0)
    def _():
        a_scr[...] = _norm_mod(h_ref[...], g_ref[...], sh_ref[...], sc_ref[...]).astype(BF16)
        acc_scr[...] = jnp.zeros_like(acc_scr)

    a = a_scr[...]
    gate = jnp.dot(a, wg_ref[...], preferred_element_type=F32)
    up = jnp.dot(a, wu_ref[...], preferred_element_type=F32)
    act = (_silu(gate) * up).astype(BF16)
    acc_scr[...] += jnp.dot(act, wd_ref[...], preferred_element_type=F32)

    @pl.when(f == pl.num_programs(1) - 1)
    def _():
        o_ref[...] = h_ref[...] + gt_ref[...] * acc_scr[...]


def _ffn_dense(h, g, sh, sc, gate, wg, wu, wd, S, tm, tf):
    T, D = h.shape
    F = wg.shape[1]
    per_b = S // tm
    return pl.pallas_call(
        _ffn_dense_kernel,
        grid=(T // tm, F // tf),
        in_specs=[
            pl.BlockSpec((tm, D), lambda i, f: (i, 0)),
            pl.BlockSpec((1, D), lambda i, f: (0, 0)),
            pl.BlockSpec((None, 1, D), lambda i, f: (i // per_b, 0, 0)),
            pl.BlockSpec((None, 1, D), lambda i, f: (i // per_b, 0, 0)),
            pl.BlockSpec((None, 1, D), lambda i, f: (i // per_b, 0, 0)),
            pl.BlockSpec((D, tf), lambda i, f: (0, f)),
            pl.BlockSpec((D, tf), lambda i, f: (0, f)),
            pl.BlockSpec((tf, D), lambda i, f: (f, 0)),
        ],
        out_specs=pl.BlockSpec((tm, D), lambda i, f: (i, 0)),
        out_shape=jax.ShapeDtypeStruct((T, D), F32),
        scratch_shapes=[pltpu.VMEM((tm, D), BF16), pltpu.VMEM((tm, D), F32)],
        compiler_params=_params("parallel", "arbitrary"),
        name="ffn_dense",
    )(h, g, sh, sc, gate, wg, wu, wd)


def _ffn_moe_kernel(te_ref, tv_ref, x_ref, rw_ref, wg_ref, wu_ref, wd_ref, o_ref, acc_scr):
    i = pl.program_id(0)
    f = pl.program_id(1)
    valid = tv_ref[i] > 0

    @pl.when(jnp.logical_and(valid, f == 0))
    def _():
        acc_scr[...] = jnp.zeros_like(acc_scr)

    @pl.when(valid)
    def _():
        x = x_ref[...]
        gate = jnp.dot(x, wg_ref[...], preferred_element_type=F32)
        up = jnp.dot(x, wu_ref[...], preferred_element_type=F32)
        act = (_silu(gate) * up).astype(BF16)
        acc_scr[...] += jnp.dot(act, wd_ref[...], preferred_element_type=F32)

    @pl.when(f == pl.num_programs(1) - 1)
    def _():
        o_ref[...] = jnp.where(valid, acc_scr[...] * rw_ref[...], 0.0)


def _ffn_moe(tile_expert, tile_valid, xs, rw, wg, wu, wd, tm, tf):
    P, D = xs.shape
    E, _, F = wg.shape
    nf = F // tf

    def wcol(i, f, te, tv):
        return (te[i], 0, jnp.where(tv[i] > 0, f, nf - 1))

    def wrow(i, f, te, tv):
        return (te[i], jnp.where(tv[i] > 0, f, nf - 1), 0)

    grid_spec = pltpu.PrefetchScalarGridSpec(
        num_scalar_prefetch=2,
        grid=(P // tm, nf),
        in_specs=[
            pl.BlockSpec((tm, D), lambda i, f, te, tv: (i, 0)),
            pl.BlockSpec((tm, 1), lambda i, f, te, tv: (i, 0)),
            pl.BlockSpec((None, D, tf), wcol),
            pl.BlockSpec((None, D, tf), wcol),
            pl.BlockSpec((None, tf, D), wrow),
        ],
        out_specs=pl.BlockSpec((tm, D), lambda i, f, te, tv: (i, 0)),
        scratch_shapes=[pltpu.VMEM((tm, D), F32)],
    )
    return pl.pallas_call(
        _ffn_moe_kernel,
        grid_spec=grid_spec,
        out_shape=jax.ShapeDtypeStruct((P, D), F32),
        compiler_params=_params("parallel", "arbitrary"),
        name="ffn_moe",
    )(tile_expert, tile_valid, xs, rw, wg, wu, wd)


def _router_kernel(h_ref, g_ref, sh_ref, sc_ref, rw_ref, rb_ref, m_ref, idx_ref, wt_ref, *, n_e):
    a = _norm_mod(h_ref[...], g_ref[...], sh_ref[...], sc_ref[...]).astype(BF16)
    m_ref[...] = a
    logits = jnp.dot(a, rw_ref[...], preferred_element_type=F32) + rb_ref[...]
    lane = lax.broadcasted_iota(jnp.int32, logits.shape, 1)
    lane_f = lane.astype(F32)
    logits = jnp.where(lane < n_e, logits, NEG_BIG)
    m1 = jnp.max(logits, axis=-1, keepdims=True)
    i1 = jnp.min(jnp.where(logits == m1, lane_f, float(LANES)), axis=-1, keepdims=True)
    rest = jnp.where(lane_f == i1, NEG_BIG, logits)
    m2 = jnp.max(rest, axis=-1, keepdims=True)
    i2 = jnp.min(jnp.where(rest == m2, lane_f, float(LANES)), axis=-1, keepdims=True)
    e2 = jnp.exp(m2 - m1)
    w1 = 1.0 / (1.0 + e2)
    w2 = e2 * w1
    w = jnp.where(lane == 0, w1, jnp.where(lane == 1, w2, 0.0))
    ii = jnp.where(lane == 0, i1, jnp.where(lane == 1, i2, 0.0)).astype(jnp.int32)
    wt_ref[...] = w[:, :8]
    idx_ref[...] = ii[:, :8]


def _router(h, g, sh, sc, rw, rb, S, tm, n_e):
    T, D = h.shape
    per_b = S // tm
    return pl.pallas_call(
        functools.partial(_router_kernel, n_e=n_e),
        grid=(T // tm,),
        in_specs=[
            pl.BlockSpec((tm, D), lambda i: (i, 0)),
            pl.BlockSpec((1, D), lambda i: (0, 0)),
            pl.BlockSpec((None, 1, D), lambda i: (i // per_b, 0, 0)),
            pl.BlockSpec((None, 1, D), lambda i: (i // per_b, 0, 0)),
            pl.BlockSpec((D, LANES), lambda i: (0, 0)),
            pl.BlockSpec((1, LANES), lambda i: (0, 0)),
        ],
        out_specs=[
            pl.BlockSpec((tm, D), lambda i: (i, 0)),
            pl.BlockSpec((tm, 8), lambda i: (i, 0)),
            pl.BlockSpec((tm, 8), lambda i: (i, 0)),
        ],
        out_shape=[
            jax.ShapeDtypeStruct((T, D), BF16),
            jax.ShapeDtypeStruct((T, 8), jnp.int32),
            jax.ShapeDtypeStruct((T, 8), F32),
        ],
        compiler_params=_params("parallel"),
        name="router",
    )(h, g, sh, sc, rw, rb)


def _forget_kernel(h_ref, g_ref, sh_ref, sc_ref, wf_ref, bf_ref, o_ref, carry_ref, *, per_b):
    @pl.when(pl.program_id(0) % per_b == 0)
    def _():
        carry_ref[...] = jnp.zeros_like(carry_ref)

    a = _norm_mod(h_ref[...], g_ref[...], sh_ref[...], sc_ref[...]).astype(BF16)
    z = jnp.dot(a, wf_ref[...], preferred_element_type=F32) + bf_ref[...]
    lf = jnp.minimum(z, 0.0) - jnp.log(1.0 + jnp.exp(-jnp.abs(z)))
    tm = lf.shape[0]
    r = lax.broadcasted_iota(jnp.int32, (tm, tm), 0)
    c = lax.broadcasted_iota(jnp.int32, (tm, tm), 1)
    tri = jnp.where(c <= r, 1.0, 0.0).astype(F32)
    cs = jnp.dot(tri, lf, preferred_element_type=F32, precision=lax.Precision.HIGHEST)
    out = cs + carry_ref[...]
    carry_ref[...] = out[tm - 1:tm, :]
    hi = out.astype(BF16)
    r1 = out - hi.astype(F32)
    mid = r1.astype(BF16)
    lo = (r1 - mid.astype(F32)).astype(BF16)
    o_ref[:, 0:LANES] = hi
    o_ref[:, LANES:2 * LANES] = mid
    o_ref[:, 2 * LANES:3 * LANES] = lo


def _forget_cumsum(h, g, sh, sc, wf, bf, S, tm):
    T, D = h.shape
    per_b = S // tm
    return pl.pallas_call(
        functools.partial(_forget_kernel, per_b=per_b),
        grid=(T // tm,),
        in_specs=[
            pl.BlockSpec((tm, D), lambda i: (i, 0)),
            pl.BlockSpec((1, D), lambda i: (0, 0)),
            pl.BlockSpec((None, 1, D), lambda i: (i // per_b, 0, 0)),
            pl.BlockSpec((None, 1, D), lambda i: (i // per_b, 0, 0)),
            pl.BlockSpec((D, LANES), lambda i: (0, 0)),
            pl.BlockSpec((1, LANES), lambda i: (0, 0)),
        ],
        out_specs=pl.BlockSpec((tm, 3 * LANES), lambda i: (i, 0)),
        out_shape=jax.ShapeDtypeStruct((T, 3 * LANES), BF16),
        scratch_shapes=[pltpu.VMEM((1, LANES), F32)],
        compiler_params=_params("arbitrary"),
        name="forget_cumsum",
    )(h, g, sh, sc, wf, bf)


def _fox_kernel(qt_ref, k_ref, vt_ref, o_ref, *, tq, tk):
    i = pl.program_id(2)
    qt = qt_ref[...]
    dh = vt_ref.shape[0]

    def step(kb, carry, masked):
        m, l, acc = carry
        k0 = pl.multiple_of(kb * tk, tk)
        s = jnp.dot(k_ref[pl.ds(k0, tk), :], qt, preferred_element_type=F32)
        if masked:
            key = lax.broadcasted_iota(jnp.int32, (tk, tq), 0)
            qry = lax.broadcasted_iota(jnp.int32, (tk, tq), 1)
            s = jnp.where(key <= qry, s, NEG_BIG)
        m_new = jnp.maximum(m, jnp.max(s, axis=0, keepdims=True))
        alpha = jnp.exp(m - m_new)
        p = jnp.exp(s - m_new)
        l = alpha * l + jnp.sum(p, axis=0, keepdims=True)
        pv = jnp.dot(vt_ref[:, pl.ds(k0, tk)], p.astype(BF16), preferred_element_type=F32)
        return m_new, l, alpha * acc + pv

    init = (jnp.full((1, tq), NEG_BIG, F32), jnp.zeros((1, tq), F32), jnp.zeros((dh, tq), F32))
    carry = lax.fori_loop(0, i, lambda kb, c: step(kb, c, False), init)
    m, l, acc = step(i, carry, True)
    o_ref[...] = acc * (1.0 / l)


def _fox_attention(qt, ka, vt, tq):
    B, H, _, S = qt.shape
    dh = vt.shape[2]
    return pl.pallas_call(
        functools.partial(_fox_kernel, tq=tq, tk=tq),
        grid=(B, H, S // tq),
        in_specs=[
            pl.BlockSpec((None, None, LANES, tq), lambda b, h, i: (b, h, 0, i)),
            pl.BlockSpec((None, None, S, LANES), lambda b, h, i: (b, h, 0, 0)),
            pl.BlockSpec((None, None, dh, S), lambda b, h, i: (b, h, 0, 0)),
        ],
        out_specs=pl.BlockSpec((None, None, dh, tq), lambda b, h, i: (b, h, 0, i)),
        out_shape=jax.ShapeDtypeStruct((B, H, dh, S), F32),
        compiler_params=_params("parallel", "parallel", "arbitrary"),
        name="fox_attention",
    )(qt, ka, vt)


def _final_kernel(h_ref, g_ref, sh_ref, sc_ref, o_ref):
    o_ref[...] = _norm_mod(h_ref[...], g_ref[...], sh_ref[...], sc_ref[...])


def _final_norm(h, g, sh, sc, S, tm):
    T, D = h.shape
    per_b = S // tm
    return pl.pallas_call(
        _final_kernel,
        grid=(T // tm,),
        in_specs=[
            pl.BlockSpec((tm, D), lambda i: (i, 0)),
            pl.BlockSpec((1, D), lambda i: (0, 0)),
            pl.BlockSpec((None, 1, D), lambda i: (i // per_b, 0, 0)),
            pl.BlockSpec((None, 1, D), lambda i: (i // per_b, 0, 0)),
        ],
        out_specs=pl.BlockSpec((tm, D), lambda i: (i, 0)),
        out_shape=jax.ShapeDtypeStruct((T, D), F32),
        compiler_params=_params("parallel"),
        name="final_norm",
    )(h, g, sh, sc)


def _dispatch_tables(top_idx, n_e, tm):
    T, K = top_idx.shape
    n = T * K
    n_tiles = n // tm + n_e
    e_flat = top_idx.reshape(n)
    order = jnp.argsort(e_flat, stable=True).astype(jnp.int32)
    counts = jnp.sum(e_flat[:, None] == jnp.arange(n_e, dtype=jnp.int32)[None, :], axis=0).astype(jnp.int32)
    starts = jnp.cumsum(counts) - counts
    tiles_per = (counts + tm - 1) // tm
    tile_end = jnp.cumsum(tiles_per)
    pad_starts = (tile_end - tiles_per) * tm
    tile_id = jnp.arange(n_tiles, dtype=jnp.int32)
    tile_expert = jnp.sum(tile_id[:, None] >= tile_end[None, :], axis=1).astype(jnp.int32)
    tile_valid = (tile_expert < n_e).astype(jnp.int32)
    last_e = jnp.max(jnp.where(counts > 0, jnp.arange(n_e, dtype=jnp.int32), 0))
    tile_expert = jnp.where(tile_valid > 0, tile_expert, last_e)
    slot = jnp.arange(n_tiles * tm, dtype=jnp.int32)
    se = tile_expert[slot // tm]
    r = slot - pad_starts[se]
    ok = jnp.logical_and(r < counts[se], tile_valid[slot // tm] > 0)
    src_flat = order[jnp.clip(starts[se] + r, 0, n - 1)]
    src_flat = jnp.where(ok, src_flat, 0)
    e_sorted = e_flat[order]
    dest_sorted = pad_starts[e_sorted] + (jnp.arange(n, dtype=jnp.int32) - starts[e_sorted])
    pos = jnp.zeros((n,), jnp.int32).at[order].set(dest_sorted).reshape(T, K)
    return src_flat, ok, pos, tile_expert, tile_valid


def _tile(S, want):
    t = min(S, want)
    assert S % t == 0
    return t


def kernel(x, c, ada_w, ada_b, norm_g, ret_w_in, ret_w_o, kv_ada_w, kv_ada_b, kv_norm_g, fox_w_kv, fox_w_f, fox_b_f, fox_w_qg, fox_w_o, ffn_w_gate, ffn_w_up, ffn_w_down, router_w, router_b, moe_w_gate, moe_w_up, moe_w_down, final_ada_w, final_ada_b, final_norm_g):
    B, S, D = x.shape
    T = B * S
    depth = ada_w.shape[0]
    n_a = ret_w_in.shape[0]
    n_e = router_w.shape[-1]
    dh = D // FOX_HEADS
    dk = D // RET_HEADS
    assert dk == 2 * LANES and S % CHUNK == 0 and n_e <= 8

    tm = _tile(S, 512)
    tm_ffn = _tile(S, 1024)
    tm_moe = _tile(S, 1024)
    tq = _tile(S, 512)
    lc = _tile(S, 512)
    f_dense = ffn_w_gate.shape[-1]
    f_moe = moe_w_gate.shape[-1]
    tf_dense = f_dense // 2 if (f_dense // 2) % LANES == 0 else f_dense
    tf_moe = 512 if f_moe % 512 == 0 else f_moe

    c_pad = jnp.zeros((8, D), F32).at[:B].set(c)
    ada = _ada(c_pad, ada_w, ada_b[:, None, :])[:, :B]
    extra_w = jnp.stack([kv_ada_w, final_ada_w])
    extra_b = jnp.stack([kv_ada_b, final_ada_b])[:, None, :]
    extra = _ada(c_pad, extra_w, extra_b)[:, :B]
    vec = lambda a: a[:, None, :]

    h = x.reshape(T, D)
    cos, sin = _rotary_tables(S, dk)
    ret_tabs = _retention_tables()
    shared = None

    for l in range(depth):
        sh1, sc1, g1, sh2, sc2, g2 = [vec(a) for a in jnp.split(ada[l], 6, axis=-1)]
        ng1 = norm_g[l, 0][None, :]
        ng2 = norm_g[l, 1][None, :]
        if l == n_a:
            kv_sh, kv_sc = [vec(a) for a in jnp.split(extra[0], 2, axis=-1)]
            kvg = kv_norm_g[None, :]
            kv = _norm_mod_matmul(h, kvg, kv_sh, kv_sc, fox_w_kv.astype(BF16), S,
                                  modes=("plain",) * 4, scales=(1.0,) * 4, cw=2 * D // 4,
                                  out_dtype=BF16, tm=tm)
            wf = jnp.zeros((D, LANES), F32).at[:, :FOX_HEADS].set(fox_w_f).astype(BF16)
            bf = jnp.zeros((1, LANES), F32).at[0, :FOX_HEADS].set(fox_b_f)
            Fc = _forget_cumsum(h, kvg, kv_sh, kv_sc, wf, bf, S, tm)
            k4 = kv[:, :D].reshape(B, S, FOX_HEADS, dh).transpose(0, 2, 1, 3)
            vt = kv[:, D:].reshape(B, S, FOX_HEADS, dh).transpose(0, 2, 3, 1)
            f3 = Fc.reshape(B, S, 3, LANES)[:, :, :, :FOX_HEADS].transpose(0, 3, 1, 2)
            ones3 = jnp.ones_like(f3)
            padk = jnp.zeros((B, FOX_HEADS, S, LANES - dh - 6), BF16)
            ka = jnp.concatenate([k4, ones3, -f3, padk], axis=-1)
            f3t = f3.transpose(0, 1, 3, 2)
            shared = (ka, vt, f3t)
        if l < n_a:
            w_in = ret_w_in[l].astype(BF16)
            qkv = _norm_mod_matmul(
                h, ng1, sh1, sc1, w_in[:, :4 * D], S,
                modes=("rot",) * 4 + ("plain",) * 4,
                scales=(1.0,) * 2 + (dk ** -0.5,) * 2 + (1.0,) * 4,
                cw=D // 2, out_dtype=BF16, tm=tm, rot=(cos, sin))
            gate = _norm_mod_matmul(h, ng1, sh1, sc1, w_in[:, 4 * D:], S,
                                    modes=("plain",) * 4, scales=(1.0,) * 4, cw=D // 2,
                                    out_dtype=F32, tm=tm)
            y = _retention(qkv, gate, ret_tabs, B, S, lc)
            h = _proj_residual((y,), ret_w_o[l].astype(BF16), h, g1, S, tm)
        else:
            j = l - n_a
            ka, vt, f3t = shared
            qg = _norm_mod_matmul(h, ng1, sh1, sc1, fox_w_qg[j].astype(BF16), S,
                                  modes=("plain",) * 4, scales=(dh ** -0.5,) * 2 + (1.0,) * 2,
                                  cw=D // 2, out_dtype=F32, tm=tm)
            q4 = qg[:, :D].astype(BF16).reshape(B, S, FOX_HEADS, dh).transpose(0, 2, 3, 1)
            padq = jnp.zeros((B, FOX_HEADS, LANES - dh - 6, S), BF16)
            qt = jnp.concatenate([q4, f3t, jnp.ones_like(f3t), padq], axis=2)
            ot = _fox_attention(qt, ka, vt, tq)
            o = ot.transpose(0, 3, 1, 2).reshape(T, D)
            h = _proj_residual((o, qg[:, D:]), fox_w_o[j].astype(BF16), h, g1, S, tm)
        i = l // 2
        if l % 2 == 0:
            h = _ffn_dense(h, ng2, sh2, sc2, g2, ffn_w_gate[i].astype(BF16), ffn_w_up[i].astype(BF16),
                           ffn_w_down[i].astype(BF16), S, tm_ffn, tf_dense)
        else:
            rw = jnp.zeros((D, LANES), F32).at[:, :n_e].set(router_w[i]).astype(BF16)
            rb = jnp.zeros((1, LANES), F32).at[0, :n_e].set(router_b[i])
            m, idx8, wt8 = _router(h, ng2, sh2, sc2, rw, rb, S, tm, n_e)
            src, ok, pos, tile_expert, tile_valid = _dispatch_tables(idx8[:, :TOP_K], n_e, tm_moe)
            xs = jnp.take(m, src // TOP_K, axis=0)
            row_w = jnp.where(ok, jnp.take(wt8[:, :TOP_K].reshape(-1), src), 0.0)[:, None]
            ys = _ffn_moe(tile_expert, tile_valid, xs, row_w, moe_w_gate[i].astype(BF16),
                          moe_w_up[i].astype(BF16), moe_w_down[i].astype(BF16), tm_moe, tf_moe)
            ff = jnp.take(ys, pos[:, 0], axis=0) + jnp.take(ys, pos[:, 1], axis=0)
            h = h + jnp.repeat(g2[:, 0, :], S, axis=0) * ff

    f_sh, f_sc = [vec(a) for a in jnp.split(extra[1], 2, axis=-1)]
    out = _final_norm(h, final_norm_g[None, :], f_sh, f_sc, S, tm)
    return out.reshape(B, S, D)
```

```python
import functools
import math

import jax
import jax.numpy as jnp
from jax import lax
from jax.experimental import pallas as pl
from jax.experimental.pallas import tpu as pltpu

F32 = jnp.float32
BF16 = jnp.bfloat16

CHUNK = 64
RET_HEADS = 4
FOX_HEADS = 16
ROPE_BASE = 10000.0
TOP_K = 2
EPS = 1e-6

LANES = 128
VMEM_LIMIT_BYTES = 56 * 1024 * 1024
NEG_BIG = -1e30
LOG2E = math.log2(math.e)


def _params(*sem):
    return pltpu.CompilerParams(dimension_semantics=sem, vmem_limit_bytes=VMEM_LIMIT_BYTES)


def _silu(x):
    return x * (1.0 / (1.0 + jnp.exp(-x)))


def _sigmoid(x):
    return 1.0 / (1.0 + jnp.exp(-x))


def _norm_mod(x, g, sh, sc):
    var = jnp.mean(x * x, axis=-1, keepdims=True)
    y = (x * lax.rsqrt(var + EPS)) * g
    return y * (1.0 + sc) + sh


def _ada_kernel(c_ref, w_ref, b_ref, o_ref):
    ca = _silu(c_ref[...]).astype(BF16)
    o_ref[...] = jnp.dot(ca, w_ref[...].astype(BF16), preferred_element_type=F32) + b_ref[...]


def _ada(c_pad, w, b, tn=1024):
    L, D, N = w.shape
    return pl.pallas_call(
        _ada_kernel,
        grid=(L, N // tn),
        in_specs=[
            pl.BlockSpec((8, D), lambda l, j: (0, 0)),
            pl.BlockSpec((None, D, tn), lambda l, j: (l, 0, j)),
            pl.BlockSpec((None, 1, tn), lambda l, j: (l, 0, j)),
        ],
        out_specs=pl.BlockSpec((None, 8, tn), lambda l, j: (l, 0, j)),
        out_shape=jax.ShapeDtypeStruct((L, 8, N), F32),
        compiler_params=_params("parallel", "parallel"),
        name="ada_proj",
    )(c_pad, w, b)


def _nmm_kernel(*refs, n_chunks, cw, modes, scales, out_dtype):
    h_ref, g_ref, sh_ref, sc_ref, w_ref = refs[:5]
    has_rot = any(m == "rot" for m in modes)
    if has_rot:
        cos_ref, sin_ref, o_ref = refs[5:8]
    else:
        o_ref = refs[5]
    a = _norm_mod(h_ref[...], g_ref[...], sh_ref[...], sc_ref[...]).astype(BF16)
    for c in range(n_chunks):
        acc = jnp.dot(a, w_ref[:, c * cw:(c + 1) * cw], preferred_element_type=F32)
        if modes[c] == "rot":
            cos = cos_ref[...]
            sin = sin_ref[...]
            parts = []
            for hh in range(cw // (2 * LANES)):
                x1 = acc[:, (2 * hh) * LANES:(2 * hh + 1) * LANES]
                x2 = acc[:, (2 * hh + 1) * LANES:(2 * hh + 2) * LANES]
                parts.append(x1 * cos - x2 * sin)
                parts.append(x2 * cos + x1 * sin)
            acc = jnp.concatenate(parts, axis=-1)
        if scales[c] != 1.0:
            acc = acc * scales[c]
        o_ref[:, c * cw:(c + 1) * cw] = acc.astype(out_dtype)


def _norm_mod_matmul(h, g, sh, sc, w, S, *, modes, scales, cw, out_dtype, tm, rot=None):
    T, D = h.shape
    N = w.shape[1]
    n_chunks = N // cw
    per_b = S // tm
    in_specs = [
        pl.BlockSpec((tm, D), lambda i: (i, 0)),
        pl.BlockSpec((1, D), lambda i: (0, 0)),
        pl.BlockSpec((None, 1, D), lambda i: (i // per_b, 0, 0)),
        pl.BlockSpec((None, 1, D), lambda i: (i // per_b, 0, 0)),
        pl.BlockSpec((D, N), lambda i: (0, 0)),
    ]
    args = [h, g, sh, sc, w]
    if rot is not None:
        in_specs += [pl.BlockSpec((tm, LANES), lambda i: (i % per_b, 0))] * 2
        args += list(rot)
    return pl.pallas_call(
        functools.partial(_nmm_kernel, n_chunks=n_chunks, cw=cw, modes=modes, scales=scales,
                          out_dtype=out_dtype),
        grid=(T // tm,),
        in_specs=in_specs,
        out_specs=pl.BlockSpec((tm, N), lambda i: (i, 0)),
        out_shape=jax.ShapeDtypeStruct((T, N), out_dtype),
        compiler_params=_params("parallel"),
        name="norm_mod_matmul",
    )(*args)


def _ret_kernel(q_ref, k_ref, v_ref, g_ref, intra_ref, qd_ref, kd_ref, cd_ref, o_ref, st_ref, *,
                n_sub):
    @pl.when(pl.program_id(2) == 0)
    def _():
        st_ref[...] = jnp.zeros_like(st_ref)

    intra = intra_ref[...]
    qd = qd_ref[...]
    kd = kd_ref[...]
    cd = cd_ref[...]

    def body(c, carry):
        r0 = pl.multiple_of(c * CHUNK, CHUNK)
        qc = q_ref[pl.ds(r0, CHUNK), :]
        kc = k_ref[pl.ds(r0, CHUNK), :]
        vc = v_ref[pl.ds(r0, CHUNK), :]
        state = st_ref[...]
        scores = lax.dot_general(qc, kc, (((1,), (1,)), ((), ())), preferred_element_type=F32) * intra
        o = (jnp.dot(scores.astype(BF16), vc, preferred_element_type=F32)
             + jnp.dot(qc, state.astype(BF16), preferred_element_type=F32) * qd)
        kdk = (kc.astype(F32) * kd).astype(BF16)
        st_ref[...] = state * cd + lax.dot_general(kdk, vc, (((0,), (0,)), ((), ())),
                                                   preferred_element_type=F32)
        mu = jnp.mean(o, axis=-1, keepdims=True)
        d = o - mu
        var = jnp.mean(d * d, axis=-1, keepdims=True)
        y = d * lax.rsqrt(var + EPS)
        o_ref[pl.ds(r0, CHUNK), :] = (_silu(g_ref[pl.ds(r0, CHUNK), :]) * y).astype(o_ref.dtype)
        return carry

    lax.fori_loop(0, n_sub, body, 0)


def _retention(qkv, gate, tabs, B, S, lc):
    T = qkv.shape[0]
    D = qkv.shape[1] // 4
    dk = D // RET_HEADS
    dv = 2 * D // RET_HEADS
    H = RET_HEADS
    per_b = S // lc
    intra, qd, kd, cd = tabs
    row = lambda b, h, s: b * per_b + s
    return pl.pallas_call(
        functools.partial(_ret_kernel, n_sub=lc // CHUNK),
        grid=(B, H, per_b),
        in_specs=[
            pl.BlockSpec((lc, dk), lambda b, h, s: (row(b, h, s), h)),
            pl.BlockSpec((lc, dk), lambda b, h, s: (row(b, h, s), H + h)),
            pl.BlockSpec((lc, dv), lambda b, h, s: (row(b, h, s), H + h)),
            pl.BlockSpec((lc, dv), lambda b, h, s: (row(b, h, s), h)),
            pl.BlockSpec((None, CHUNK, CHUNK), lambda b, h, s: (h, 0, 0)),
            pl.BlockSpec((None, CHUNK, 1), lambda b, h, s: (h, 0, 0)),
            pl.BlockSpec((None, CHUNK, 1), lambda b, h, s: (h, 0, 0)),
            pl.BlockSpec((None, 1, 1), lambda b, h, s: (h, 0, 0)),
        ],
        out_specs=pl.BlockSpec((lc, dv), lambda b, h, s: (row(b, h, s), h)),
        out_shape=jax.ShapeDtypeStruct((T, 2 * D), BF16),
        scratch_shapes=[pltpu.VMEM((dk, dv), F32)],
        compiler_params=_params("parallel", "parallel", "arbitrary"),
        name="retention",
    )(qkv, qkv, qkv, gate, intra, qd, kd, cd)


def _retention_tables():
    h = jnp.arange(RET_HEADS, dtype=F32)
    log_g = jnp.log(1.0 - 2.0 ** (-5.0 - h))
    idx = jnp.arange(CHUNK, dtype=F32)
    intra = jnp.exp(log_g[:, None, None] * jnp.abs(idx[:, None] - idx[None, :]))
    qd = jnp.exp(log_g[:, None] * (idx[None, :] + 1.0))[:, :, None]
    kd = jnp.exp(log_g[:, None] * (CHUNK - 1.0 - idx[None, :]))[:, :, None]
    cd = jnp.exp(log_g * CHUNK)[:, None, None]
    return intra, qd, kd, cd


def _rotary_tables(S, d):
    inv = 1.0 / (ROPE_BASE ** (jnp.arange(0, d, 2, dtype=F32) / d))
    ang = jnp.arange(S).astype(F32)[:, None] * inv[None, :]
    return jnp.cos(ang), jnp.sin(ang)


def _proj_res_kernel(*refs, gated):
    if gated:
        o_in, og_ref, w_ref, h_ref, gt_ref, out_ref = refs
        x = (o_in[...] * _sigmoid(og_ref[...])).astype(BF16)
    else:
        x_ref, w_ref, h_ref, gt_ref, out_ref = refs
        x = x_ref[...]
    out_ref[...] = h_ref[...] + gt_ref[...] * jnp.dot(x, w_ref[...], preferred_element_type=F32)


def _proj_residual(xs, w, h, gate, S, tm):
    T, D = h.shape
    K = w.shape[0]
    per_b = S // tm
    gated = len(xs) == 2
    in_specs = [pl.BlockSpec((tm, K), lambda i: (i, 0)) for _ in xs] + [
        pl.BlockSpec((K, D), lambda i: (0, 0)),
        pl.BlockSpec((tm, D), lambda i: (i, 0)),
        pl.BlockSpec((None, 1, D), lambda i: (i // per_b, 0, 0)),
    ]
    return pl.pallas_call(
        functools.partial(_proj_res_kernel, gated=gated),
        grid=(T // tm,),
        in_specs=in_specs,
        out_specs=pl.BlockSpec((tm, D), lambda i: (i, 0)),
        out_shape=jax.ShapeDtypeStruct((T, D), F32),
        compiler_params=_params("parallel"),
        name="proj_residual",
    )(*xs, w, h, gate)


def _ffn_dense_kernel(h_ref, g_ref, sh_ref, sc_ref, gt_ref, wg_ref, wu_ref, wd_ref, o_ref, a_scr,
                      acc_scr):
    f = pl.program_id(1)

    @pl.when(f == 0)
    def _():
        a_scr[...] = _norm_mod(h_ref[...], g_ref[...], sh_ref[...], sc_ref[...]).astype(BF16)
        acc_scr[...] = jnp.zeros_like(acc_scr)

    a = a_scr[...]
    gate = jnp.dot(a, wg_ref[...], preferred_element_type=F32)
    up = jnp.dot(a, wu_ref[...], preferred_element_type=F32)
    act = (_silu(gate) * up).astype(BF16)
    acc_scr[...] += jnp.dot(act, wd_ref[...], preferred_element_type=F32)

    @pl.when(f == pl.num_programs(1) - 1)
    def _():
        o_ref[...] = h_ref[...] + gt_ref[...] * acc_scr[...]


def _ffn_dense(h, g, sh, sc, gate, wg, wu, wd, S, tm, tf):
    T, D = h.shape
    F = wg.shape[1]
    per_b = S // tm
    return pl.pallas_call(
        _ffn_dense_kernel,
        grid=(T // tm, F // tf),
        in_specs=[
            pl.BlockSpec((tm, D), lambda i, f: (i, 0)),
            pl.BlockSpec((1, D), lambda i, f: (0, 0)),
            pl.BlockSpec((None, 1, D), lambda i, f: (i // per_b, 0, 0)),
            pl.BlockSpec((None, 1, D), lambda i, f: (i // per_b, 0, 0)),
            pl.BlockSpec((None, 1, D), lambda i, f: (i // per_b, 0, 0)),
            pl.BlockSpec((D, tf), lambda i, f: (0, f)),
            pl.BlockSpec((D, tf), lambda i, f: (0, f)),
            pl.BlockSpec((tf, D), lambda i, f: (f, 0)),
        ],
        out_specs=pl.BlockSpec((tm, D), lambda i, f: (i, 0)),
        out_shape=jax.ShapeDtypeStruct((T, D), F32),
        scratch_shapes=[pltpu.VMEM((tm, D), BF16), pltpu.VMEM((tm, D), F32)],
        compiler_params=_params("parallel", "arbitrary"),
        name="ffn_dense",
    )(h, g, sh, sc, gate, wg, wu, wd)


def _ffn_moe_kernel(te_ref, tv_ref, x_ref, rw_ref, wg_ref, wu_ref, wd_ref, o_ref, acc_scr):
    i = pl.program_id(0)
    f = pl.program_id(1)
    valid = tv_ref[i] > 0

    @pl.when(jnp.logical_and(valid, f == 0))
    def _():
        acc_scr[...] = jnp.zeros_like(acc_scr)

    @pl.when(valid)
    def _():
        x = x_ref[...]
        gate = jnp.dot(x, wg_ref[...], preferred_element_type=F32)
        up = jnp.dot(x, wu_ref[...], preferred_element_type=F32)
        act = (_silu(gate) * up).astype(BF16)
        acc_scr[...] += jnp.dot(act, wd_ref[...], preferred_element_type=F32)

    @pl.when(f == pl.num_programs(1) - 1)
    def _():
        o_ref[...] = jnp.where(valid, acc_scr[...] * rw_ref[...], 0.0)


def _ffn_moe(tile_expert, tile_valid, xs, rw, wg, wu, wd, tm, tf):
    P, D = xs.shape
    E, _, F = wg.shape
    nf = F // tf

    def wcol(i, f, te, tv):
        return (te[i], 0, jnp.where(tv[i] > 0, f, nf - 1))

    def wrow(i, f, te, tv):
        return (te[i], jnp.where(tv[i] > 0, f, nf - 1), 0)

    grid_spec = pltpu.PrefetchScalarGridSpec(
        num_scalar_prefetch=2,
        grid=(P // tm, nf),
        in_specs=[
            pl.BlockSpec((tm, D), lambda i, f, te, tv: (i, 0)),
            pl.BlockSpec((tm, 1), lambda i, f, te, tv: (i, 0)),
            pl.BlockSpec((None, D, tf), wcol),
            pl.BlockSpec((None, D, tf), wcol),
            pl.BlockSpec((None, tf, D), wrow),
        ],
        out_specs=pl.BlockSpec((tm, D), lambda i, f, te, tv: (i, 0)),
        scratch_shapes=[pltpu.VMEM((tm, D), F32)],
    )
    return pl.pallas_call(
        _ffn_moe_kernel,
        grid_spec=grid_spec,
        out_shape=jax.ShapeDtypeStruct((P, D), F32),
        compiler_params=_params("parallel", "arbitrary"),
        name="ffn_moe",
    )(tile_expert, tile_valid, xs, rw, wg, wu, wd)


def _router_kernel(h_ref, g_ref, sh_ref, sc_ref, rw_ref, rb_ref, m_ref, idx_ref, wt_ref, *, n_e):
    a = _norm_mod(h_ref[...], g_ref[...], sh_ref[...], sc_ref[...]).astype(BF16)
    m_ref[...] = a
    logits = jnp.dot(a, rw_ref[...], preferred_element_type=F32) + rb_ref[...]
    lane = lax.broadcasted_iota(jnp.int32, logits.shape, 1)
    lane_f = lane.astype(F32)
    logits = jnp.where(lane < n_e, logits, NEG_BIG)
    m1 = jnp.max(logits, axis=-1, keepdims=True)
    i1 = jnp.min(jnp.where(logits == m1, lane_f, float(LANES)), axis=-1, keepdims=True)
    rest = jnp.where(lane_f == i1, NEG_BIG, logits)
    m2 = jnp.max(rest, axis=-1, keepdims=True)
    i2 = jnp.min(jnp.where(rest == m2, lane_f, float(LANES)), axis=-1, keepdims=True)
    e2 = jnp.exp(m2 - m1)
    w1 = 1.0 / (1.0 + e2)
    w2 = e2 * w1
    w = jnp.where(lane == 0, w1, jnp.where(lane == 1, w2, 0.0))
    ii = jnp.where(lane == 0, i1, jnp.where(lane == 1, i2, 0.0)).astype(jnp.int32)
    wt_ref[...] = w[:, :8]
    idx_ref[...] = ii[:, :8]


def _router(h, g, sh, sc, rw, rb, S, tm, n_e):
    T, D = h.shape
    per_b = S // tm
    return pl.pallas_call(
        functools.partial(_router_kernel, n_e=n_e),
        grid=(T // tm,),
        in_specs=[
            pl.BlockSpec((tm, D), lambda i: (i, 0)),
            pl.BlockSpec((1, D), lambda i: (0, 0)),
            pl.BlockSpec((None, 1, D), lambda i: (i // per_b, 0, 0)),
            pl.BlockSpec((None, 1, D), lambda i: (i // per_b, 0, 0)),
            pl.BlockSpec((D, LANES), lambda i: (0, 0)),
            pl.BlockSpec((1, LANES), lambda i: (0, 0)),
        ],
        out_specs=[
            pl.BlockSpec((tm, D), lambda i: (i, 0)),
            pl.BlockSpec((tm, 8), lambda i: (i, 0)),
            pl.BlockSpec((tm, 8), lambda i: (i, 0)),
        ],
        out_shape=[
            jax.ShapeDtypeStruct((T, D), BF16),
            jax.ShapeDtypeStruct((T, 8), jnp.int32),
            jax.ShapeDtypeStruct((T, 8), F32),
        ],
        compiler_params=_params("parallel"),
        name="router",
    )(h, g, sh, sc, rw, rb)


def _forget_kernel(h_ref, g_ref, sh_ref, sc_ref, wf_ref, bf_ref, o_ref, carry_ref, *, per_b):
    @pl.when(pl.program_id(0) % per_b == 0)
    def _():
        carry_ref[...] = jnp.zeros_like(carry_ref)

    a = _norm_mod(h_ref[...], g_ref[...], sh_ref[...], sc_ref[...]).astype(BF16)
    z = jnp.dot(a, wf_ref[...], preferred_element_type=F32) + bf_ref[...]
    lf = jnp.minimum(z, 0.0) - jnp.log(1.0 + jnp.exp(-jnp.abs(z)))
    tm = lf.shape[0]
    r = lax.broadcasted_iota(jnp.int32, (tm, tm), 0)
    c = lax.broadcasted_iota(jnp.int32, (tm, tm), 1)
    tri = jnp.where(c <= r, 1.0, 0.0).astype(F32)
    cs = jnp.dot(tri, lf, preferred_element_type=F32, precision=lax.Precision.HIGHEST)
    out = cs + carry_ref[...]
    carry_ref[...] = out[tm - 1:tm, :]
    out2 = out * LOG2E
    hi = out2.astype(BF16)
    r1 = out2 - hi.astype(F32)
    mid = r1.astype(BF16)
    lo = (r1 - mid.astype(F32)).astype(BF16)
    o_ref[:, 0:LANES] = hi
    o_ref[:, LANES:2 * LANES] = mid
    o_ref[:, 2 * LANES:3 * LANES] = lo


def _forget_cumsum(h, g, sh, sc, wf, bf, S, tm):
    T, D = h.shape
    per_b = S // tm
    return pl.pallas_call(
        functools.partial(_forget_kernel, per_b=per_b),
        grid=(T // tm,),
        in_specs=[
            pl.BlockSpec((tm, D), lambda i: (i, 0)),
            pl.BlockSpec((1, D), lambda i: (0, 0)),
            pl.BlockSpec((None, 1, D), lambda i: (i // per_b, 0, 0)),
            pl.BlockSpec((None, 1, D), lambda i: (i // per_b, 0, 0)),
            pl.BlockSpec((D, LANES), lambda i: (0, 0)),
            pl.BlockSpec((1, LANES), lambda i: (0, 0)),
        ],
        out_specs=pl.BlockSpec((tm, 3 * LANES), lambda i: (i, 0)),
        out_shape=jax.ShapeDtypeStruct((T, 3 * LANES), BF16),
        scratch_shapes=[pltpu.VMEM((1, LANES), F32)],
        compiler_params=_params("arbitrary"),
        name="forget_cumsum",
    )(h, g, sh, sc, wf, bf)


def _fox_kernel(qt_ref, k_ref, vt_ref, o_ref, s0_ref, s1_ref, *, tq, tk):
    i = pl.program_id(2)
    qt = qt_ref[...]
    dh = o_ref.shape[0]
    dv = vt_ref.shape[0]

    def scores(kb):
        k0 = pl.multiple_of(kb * tk, tk)
        return jnp.dot(k_ref[pl.ds(k0, tk), :], qt, preferred_element_type=F32)

    def update(s_ref, kb, m, acc):
        k0 = pl.multiple_of(kb * tk, tk)
        s = s_ref[...]
        m_new = jnp.maximum(m, jnp.max(s, axis=0, keepdims=True))
        alpha = jnp.exp2(m - m_new)
        p = jnp.exp2(s - m_new).astype(BF16)
        pv = jnp.dot(vt_ref[:, pl.ds(k0, tk)], p, preferred_element_type=F32)
        return m_new, alpha * acc + pv

    n_pairs = (i + 2) // 2
    key = lax.broadcasted_iota(jnp.int32, (tk, tq), 0)
    qry = lax.broadcasted_iota(jnp.int32, (tk, tq), 1)
    s0_ref[...] = jnp.where(key <= qry, scores(i), NEG_BIG)

    def pair(u, carry):
        m, acc = carry
        s1_ref[...] = scores(2 * u)
        m, acc = update(s0_ref, jnp.where(u == 0, i, 2 * u - 1), m, acc)
        s0_ref[...] = scores(2 * u + 1)
        return update(s1_ref, 2 * u, m, acc)

    init = (jnp.full((1, tq), NEG_BIG, F32), jnp.zeros((dv, tq), F32))
    m, acc = lax.fori_loop(0, n_pairs - 1, pair, init)
    u = n_pairs - 1
    last = jnp.minimum(2 * u, i)
    s1_ref[...] = jnp.where(2 * u < i, scores(last), NEG_BIG)
    m, acc = update(s0_ref, jnp.where(u == 0, i, 2 * u - 1), m, acc)
    m, acc = update(s1_ref, last, m, acc)
    o_ref[...] = acc[:dh, :] * (1.0 / acc[dh:dh + 1, :])


def _fox_attention(qt, ka, vt, tq, dh):
    B, H, _, S = qt.shape
    dv = vt.shape[2]
    return pl.pallas_call(
        functools.partial(_fox_kernel, tq=tq, tk=tq),
        grid=(B, H, S // tq),
        in_specs=[
            pl.BlockSpec((None, None, LANES, tq), lambda b, h, i: (b, h, 0, i)),
            pl.BlockSpec((None, None, S, LANES), lambda b, h, i: (b, h, 0, 0)),
            pl.BlockSpec((None, None, dv, S), lambda b, h, i: (b, h, 0, 0)),
        ],
        out_specs=pl.BlockSpec((None, None, dh, tq), lambda b, h, i: (b, h, 0, i)),
        out_shape=jax.ShapeDtypeStruct((B, H, dh, S), F32),
        scratch_shapes=[pltpu.VMEM((tq, tq), F32), pltpu.VMEM((tq, tq), F32)],
        compiler_params=_params("parallel", "parallel", "arbitrary"),
        name="fox_attention",
    )(qt, ka, vt)


def _final_kernel(h_ref, g_ref, sh_ref, sc_ref, o_ref):
    o_ref[...] = _norm_mod(h_ref[...], g_ref[...], sh_ref[...], sc_ref[...])


def _final_norm(h, g, sh, sc, S, tm):
    T, D = h.shape
    per_b = S // tm
    return pl.pallas_call(
        _final_kernel,
        grid=(T // tm,),
        in_specs=[
            pl.BlockSpec((tm, D), lambda i: (i, 0)),
            pl.BlockSpec((1, D), lambda i: (0, 0)),
            pl.BlockSpec((None, 1, D), lambda i: (i // per_b, 0, 0)),
            pl.BlockSpec((None, 1, D), lambda i: (i // per_b, 0, 0)),
        ],
        out_specs=pl.BlockSpec((tm, D), lambda i: (i, 0)),
        out_shape=jax.ShapeDtypeStruct((T, D), F32),
        compiler_params=_params("parallel"),
        name="final_norm",
    )(h, g, sh, sc)


def _dispatch_tables(top_idx, n_e, tm):
    T, K = top_idx.shape
    n = T * K
    n_tiles = n // tm + n_e
    e_flat = top_idx.reshape(n)
    order = jnp.argsort(e_flat, stable=True).astype(jnp.int32)
    counts = jnp.sum(e_flat[:, None] == jnp.arange(n_e, dtype=jnp.int32)[None, :], axis=0).astype(jnp.int32)
    starts = jnp.cumsum(counts) - counts
    tiles_per = (counts + tm - 1) // tm
    tile_end = jnp.cumsum(tiles_per)
    pad_starts = (tile_end - tiles_per) * tm
    tile_id = jnp.arange(n_tiles, dtype=jnp.int32)
    tile_expert = jnp.sum(tile_id[:, None] >= tile_end[None, :], axis=1).astype(jnp.int32)
    tile_valid = (tile_expert < n_e).astype(jnp.int32)
    last_e = jnp.max(jnp.where(counts > 0, jnp.arange(n_e, dtype=jnp.int32), 0))
    tile_expert = jnp.where(tile_valid > 0, tile_expert, last_e)
    slot = jnp.arange(n_tiles * tm, dtype=jnp.int32)
    se = tile_expert[slot // tm]
    r = slot - pad_starts[se]
    ok = jnp.logical_and(r < counts[se], tile_valid[slot // tm] > 0)
    src_flat = order[jnp.clip(starts[se] + r, 0, n - 1)]
    src_flat = jnp.where(ok, src_flat, 0)
    e_sorted = e_flat[order]
    dest_sorted = pad_starts[e_sorted] + (jnp.arange(n, dtype=jnp.int32) - starts[e_sorted])
    pos = jnp.zeros((n,), jnp.int32).at[order].set(dest_sorted).reshape(T, K)
    return src_flat, ok, pos, tile_expert, tile_valid


def _tile(S, want):
    t = min(S, want)
    assert S % t == 0
    return t


def kernel(x, c, ada_w, ada_b, norm_g, ret_w_in, ret_w_o, kv_ada_w, kv_ada_b, kv_norm_g, fox_w_kv, fox_w_f, fox_b_f, fox_w_qg, fox_w_o, ffn_w_gate, ffn_w_up, ffn_w_down, router_w, router_b, moe_w_gate, moe_w_up, moe_w_down, final_ada_w, final_ada_b, final_norm_g):
    B, S, D = x.shape
    T = B * S
    depth = ada_w.shape[0]
    n_a = ret_w_in.shape[0]
    n_e = router_w.shape[-1]
    dh = D // FOX_HEADS
    dk = D // RET_HEADS
    assert dk == 2 * LANES and S % CHUNK == 0 and n_e <= 8

    tm = _tile(S, 512)
    tm_ffn = _tile(S, 1024)
    tm_moe = _tile(S, 1024)
    tq = _tile(S, 512)
    lc = _tile(S, 512)
    f_dense = ffn_w_gate.shape[-1]
    f_moe = moe_w_gate.shape[-1]
    tf_dense = f_dense // 2 if (f_dense // 2) % LANES == 0 else f_dense
    tf_moe = 512 if f_moe % 512 == 0 else f_moe

    c_pad = jnp.zeros((8, D), F32).at[:B].set(c)
    ada = _ada(c_pad, ada_w, ada_b[:, None, :])[:, :B]
    extra_w = jnp.stack([kv_ada_w, final_ada_w])
    extra_b = jnp.stack([kv_ada_b, final_ada_b])[:, None, :]
    extra = _ada(c_pad, extra_w, extra_b)[:, :B]
    vec = lambda a: a[:, None, :]

    h = x.reshape(T, D)
    cos, sin = _rotary_tables(S, dk)
    ret_tabs = _retention_tables()
    shared = None

    for l in range(depth):
        sh1, sc1, g1, sh2, sc2, g2 = [vec(a) for a in jnp.split(ada[l], 6, axis=-1)]
        ng1 = norm_g[l, 0][None, :]
        ng2 = norm_g[l, 1][None, :]
        if l == n_a:
            kv_sh, kv_sc = [vec(a) for a in jnp.split(extra[0], 2, axis=-1)]
            kvg = kv_norm_g[None, :]
            kv = _norm_mod_matmul(h, kvg, kv_sh, kv_sc, fox_w_kv.astype(BF16), S,
                                  modes=("plain",) * 4, scales=(1.0,) * 4, cw=2 * D // 4,
                                  out_dtype=BF16, tm=tm)
            wf = jnp.zeros((D, LANES), F32).at[:, :FOX_HEADS].set(fox_w_f).astype(BF16)
            bf = jnp.zeros((1, LANES), F32).at[0, :FOX_HEADS].set(fox_b_f)
            Fc = _forget_cumsum(h, kvg, kv_sh, kv_sc, wf, bf, S, tm)
            k4 = kv[:, :D].reshape(B, S, FOX_HEADS, dh).transpose(0, 2, 1, 3)
            vt = kv[:, D:].reshape(B, S, FOX_HEADS, dh).transpose(0, 2, 3, 1)
            ones_row = jnp.ones((B, FOX_HEADS, 1, S), BF16)
            vt = jnp.concatenate([vt, ones_row, jnp.zeros((B, FOX_HEADS, 15, S), BF16)], axis=2)
            f3 = Fc.reshape(B, S, 3, LANES)[:, :, :, :FOX_HEADS].transpose(0, 3, 1, 2)
            ones3 = jnp.ones_like(f3)
            padk = jnp.zeros((B, FOX_HEADS, S, LANES - dh - 6), BF16)
            ka = jnp.concatenate([k4, ones3, -f3, padk], axis=-1)
            f3t = f3.transpose(0, 1, 3, 2)
            shared = (ka, vt, f3t)
        if l < n_a:
            w_in = ret_w_in[l].astype(BF16)
            qkv = _norm_mod_matmul(
                h, ng1, sh1, sc1, w_in[:, :4 * D], S,
                modes=("rot",) * 4 + ("plain",) * 4,
                scales=(1.0,) * 2 + (dk ** -0.5,) * 2 + (1.0,) * 4,
                cw=D // 2, out_dtype=BF16, tm=tm, rot=(cos, sin))
            gate = _norm_mod_matmul(h, ng1, sh1, sc1, w_in[:, 4 * D:], S,
                                    modes=("plain",) * 4, scales=(1.0,) * 4, cw=D // 2,
                                    out_dtype=F32, tm=tm)
            y = _retention(qkv, gate, ret_tabs, B, S, lc)
            h = _proj_residual((y,), ret_w_o[l].astype(BF16), h, g1, S, tm)
        else:
            j = l - n_a
            ka, vt, f3t = shared
            qg = _norm_mod_matmul(h, ng1, sh1, sc1, fox_w_qg[j].astype(BF16), S,
                                  modes=("plain",) * 4, scales=(dh ** -0.5 * LOG2E,) * 2 + (1.0,) * 2,
                                  cw=D // 2, out_dtype=F32, tm=tm)
            q4 = qg[:, :D].astype(BF16).reshape(B, S, FOX_HEADS, dh).transpose(0, 2, 3, 1)
            padq = jnp.zeros((B, FOX_HEADS, LANES - dh - 6, S), BF16)
            qt = jnp.concatenate([q4, f3t, jnp.ones_like(f3t), padq], axis=2)
            ot = _fox_attention(qt, ka, vt, tq, dh)
            o = ot.transpose(0, 3, 1, 2).reshape(T, D)
            h = _proj_residual((o, qg[:, D:]), fox_w_o[j].astype(BF16), h, g1, S, tm)
        i = l // 2
        if l % 2 == 0:
            h = _ffn_dense(h, ng2, sh2, sc2, g2, ffn_w_gate[i].astype(BF16), ffn_w_up[i].astype(BF16),
                           ffn_w_down[i].astype(BF16), S, tm_ffn, tf_dense)
        else:
            rw = jnp.zeros((D, LANES), F32).at[:, :n_e].set(router_w[i]).astype(BF16)
            rb = jnp.zeros((1, LANES), F32).at[0, :n_e].set(router_b[i])
            m, idx8, wt8 = _router(h, ng2, sh2, sc2, rw, rb, S, tm, n_e)
            src, ok, pos, tile_expert, tile_valid = _dispatch_tables(idx8[:, :TOP_K], n_e, tm_moe)
            xs = jnp.take(m, src // TOP_K, axis=0)
            row_w = jnp.where(ok, jnp.take(wt8[:, :TOP_K].reshape(-1), src), 0.0)[:, None]
            ys = _ffn_moe(tile_expert, tile_valid, xs, row_w, moe_w_gate[i].astype(BF16),
                          moe_w_up[i].astype(BF16), moe_w_down[i].astype(BF16), tm_moe, tf_moe)
            ff = jnp.take(ys, pos[:, 0], axis=0) + jnp.take(ys, pos[:, 1], axis=0)
            h = h + jnp.repeat(g2[:, 0, :], S, axis=0) * ff

    f_sh, f_sc = [vec(a) for a in jnp.split(extra[1], 2, axis=-1)]
    out = _final_norm(h, final_norm_g[None, :], f_sh, f_sc, S, tm)
    return out.reshape(B, S, D)
```

```python
import functools
import math

import jax
import jax.numpy as jnp
from jax import lax
from jax.experimental import pallas as pl
from jax.experimental.pallas import tpu as pltpu

F32 = jnp.float32
BF16 = jnp.bfloat16

CHUNK = 64
RET_HEADS = 4
FOX_HEADS = 16
ROPE_BASE = 10000.0
TOP_K = 2
EPS = 1e-6

LANES = 128
VMEM_LIMIT_BYTES = 56 * 1024 * 1024
NEG_BIG = -1e30
DV_ROWS = 80
SKIP_LOG2 = 160.0
NORM_SLACK = 1.02
LOG2E = math.log2(math.e)


def _params(*sem):
    return pltpu.CompilerParams(dimension_semantics=sem, vmem_limit_bytes=VMEM_LIMIT_BYTES)


def _silu(x):
    return x * (1.0 / (1.0 + jnp.exp(-x)))


def _sigmoid(x):
    return 1.0 / (1.0 + jnp.exp(-x))


def _norm_mod(x, g, sh, sc):
    var = jnp.mean(x * x, axis=-1, keepdims=True)
    y = (x * lax.rsqrt(var + EPS)) * g
    return y * (1.0 + sc) + sh


def _ada_kernel(c_ref, w_ref, b_ref, o_ref):
    ca = _silu(c_ref[...]).astype(BF16)
    o_ref[...] = jnp.dot(ca, w_ref[...].astype(BF16), preferred_element_type=F32) + b_ref[...]


def _ada(c_pad, w, b, tn=1024):
    L, D, N = w.shape
    return pl.pallas_call(
        _ada_kernel,
        grid=(L, N // tn),
        in_specs=[
            pl.BlockSpec((8, D), lambda l, j: (0, 0)),
            pl.BlockSpec((None, D, tn), lambda l, j: (l, 0, j)),
            pl.BlockSpec((None, 1, tn), lambda l, j: (l, 0, j)),
        ],
        out_specs=pl.BlockSpec((None, 8, tn), lambda l, j: (l, 0, j)),
        out_shape=jax.ShapeDtypeStruct((L, 8, N), F32),
        compiler_params=_params("parallel", "parallel"),
        name="ada_proj",
    )(c_pad, w, b)


def _nmm_kernel(*refs, n_chunks, cw, modes, scales, out_dtype):
    h_ref, g_ref, sh_ref, sc_ref, w_ref = refs[:5]
    has_rot = any(m == "rot" for m in modes)
    if has_rot:
        cos_ref, sin_ref, o_ref = refs[5:8]
    else:
        o_ref = refs[5]
    a = _norm_mod(h_ref[...], g_ref[...], sh_ref[...], sc_ref[...]).astype(BF16)
    for c in range(n_chunks):
        acc = jnp.dot(a, w_ref[:, c * cw:(c + 1) * cw], preferred_element_type=F32)
        if modes[c] == "rot":
            cos = cos_ref[...]
            sin = sin_ref[...]
            parts = []
            for hh in range(cw // (2 * LANES)):
                x1 = acc[:, (2 * hh) * LANES:(2 * hh + 1) * LANES]
                x2 = acc[:, (2 * hh + 1) * LANES:(2 * hh + 2) * LANES]
                parts.append(x1 * cos - x2 * sin)
                parts.append(x2 * cos + x1 * sin)
            acc = jnp.concatenate(parts, axis=-1)
        if scales[c] != 1.0:
            acc = acc * scales[c]
        o_ref[:, c * cw:(c + 1) * cw] = acc.astype(out_dtype)


def _norm_mod_matmul(h, g, sh, sc, w, S, *, modes, scales, cw, out_dtype, tm, rot=None):
    T, D = h.shape
    N = w.shape[1]
    n_chunks = N // cw
    per_b = S // tm
    in_specs = [
        pl.BlockSpec((tm, D), lambda i: (i, 0)),
        pl.BlockSpec((1, D), lambda i: (0, 0)),
        pl.BlockSpec((None, 1, D), lambda i: (i // per_b, 0, 0)),
        pl.BlockSpec((None, 1, D), lambda i: (i // per_b, 0, 0)),
        pl.BlockSpec((D, N), lambda i: (0, 0)),
    ]
    args = [h, g, sh, sc, w]
    if rot is not None:
        in_specs += [pl.BlockSpec((tm, LANES), lambda i: (i % per_b, 0))] * 2
        args += list(rot)
    return pl.pallas_call(
        functools.partial(_nmm_kernel, n_chunks=n_chunks, cw=cw, modes=modes, scales=scales,
                          out_dtype=out_dtype),
        grid=(T // tm,),
        in_specs=in_specs,
        out_specs=pl.BlockSpec((tm, N), lambda i: (i, 0)),
        out_shape=jax.ShapeDtypeStruct((T, N), out_dtype),
        compiler_params=_params("parallel"),
        name="norm_mod_matmul",
    )(*args)


def _ret_kernel(q_ref, k_ref, v_ref, g_ref, intra_ref, qd_ref, kd_ref, cd_ref, o_ref, st_ref, *,
                n_sub):
    @pl.when(pl.program_id(2) == 0)
    def _():
        st_ref[...] = jnp.zeros_like(st_ref)

    intra = intra_ref[...]
    qd = qd_ref[...]
    kd = kd_ref[...]
    cd = cd_ref[...]

    def body(c, carry):
        r0 = pl.multiple_of(c * CHUNK, CHUNK)
        qc = q_ref[pl.ds(r0, CHUNK), :]
        kc = k_ref[pl.ds(r0, CHUNK), :]
        vc = v_ref[pl.ds(r0, CHUNK), :]
        state = st_ref[...]
        scores = lax.dot_general(qc, kc, (((1,), (1,)), ((), ())), preferred_element_type=F32) * intra
        o = (jnp.dot(scores.astype(BF16), vc, preferred_element_type=F32)
             + jnp.dot(qc, state.astype(BF16), preferred_element_type=F32) * qd)
        kdk = (kc.astype(F32) * kd).astype(BF16)
        st_ref[...] = state * cd + lax.dot_general(kdk, vc, (((0,), (0,)), ((), ())),
                                                   preferred_element_type=F32)
        mu = jnp.mean(o, axis=-1, keepdims=True)
        d = o - mu
        var = jnp.mean(d * d, axis=-1, keepdims=True)
        y = d * lax.rsqrt(var + EPS)
        o_ref[pl.ds(r0, CHUNK), :] = (_silu(g_ref[pl.ds(r0, CHUNK), :]) * y).astype(o_ref.dtype)
        return carry

    lax.fori_loop(0, n_sub, body, 0)


def _retention(qkv, gate, tabs, B, S, lc):
    T = qkv.shape[0]
    D = qkv.shape[1] // 4
    dk = D // RET_HEADS
    dv = 2 * D // RET_HEADS
    H = RET_HEADS
    per_b = S // lc
    intra, qd, kd, cd = tabs
    row = lambda b, h, s: b * per_b + s
    return pl.pallas_call(
        functools.partial(_ret_kernel, n_sub=lc // CHUNK),
        grid=(B, H, per_b),
        in_specs=[
            pl.BlockSpec((lc, dk), lambda b, h, s: (row(b, h, s), h)),
            pl.BlockSpec((lc, dk), lambda b, h, s: (row(b, h, s), H + h)),
            pl.BlockSpec((lc, dv), lambda b, h, s: (row(b, h, s), H + h)),
            pl.BlockSpec((lc, dv), lambda b, h, s: (row(b, h, s), h)),
            pl.BlockSpec((None, CHUNK, CHUNK), lambda b, h, s: (h, 0, 0)),
            pl.BlockSpec((None, CHUNK, 1), lambda b, h, s: (h, 0, 0)),
            pl.BlockSpec((None, CHUNK, 1), lambda b, h, s: (h, 0, 0)),
            pl.BlockSpec((None, 1, 1), lambda b, h, s: (h, 0, 0)),
        ],
        out_specs=pl.BlockSpec((lc, dv), lambda b, h, s: (row(b, h, s), h)),
        out_shape=jax.ShapeDtypeStruct((T, 2 * D), BF16),
        scratch_shapes=[pltpu.VMEM((dk, dv), F32)],
        compiler_params=_params("parallel", "parallel", "arbitrary"),
        name="retention",
    )(qkv, qkv, qkv, gate, intra, qd, kd, cd)


def _retention_tables():
    h = jnp.arange(RET_HEADS, dtype=F32)
    log_g = jnp.log(1.0 - 2.0 ** (-5.0 - h))
    idx = jnp.arange(CHUNK, dtype=F32)
    intra = jnp.exp(log_g[:, None, None] * jnp.abs(idx[:, None] - idx[None, :]))
    qd = jnp.exp(log_g[:, None] * (idx[None, :] + 1.0))[:, :, None]
    kd = jnp.exp(log_g[:, None] * (CHUNK - 1.0 - idx[None, :]))[:, :, None]
    cd = jnp.exp(log_g * CHUNK)[:, None, None]
    return intra, qd, kd, cd


def _rotary_tables(S, d):
    inv = 1.0 / (ROPE_BASE ** (jnp.arange(0, d, 2, dtype=F32) / d))
    ang = jnp.arange(S).astype(F32)[:, None] * inv[None, :]
    return jnp.cos(ang), jnp.sin(ang)


def _proj_res_kernel(*refs, gated):
    if gated:
        o_in, og_ref, w_ref, h_ref, gt_ref, out_ref = refs
        x = (o_in[...] * _sigmoid(og_ref[...])).astype(BF16)
    else:
        x_ref, w_ref, h_ref, gt_ref, out_ref = refs
        x = x_ref[...]
    out_ref[...] = h_ref[...] + gt_ref[...] * jnp.dot(x, w_ref[...], preferred_element_type=F32)


def _proj_residual(x, w, h, gate, S, tm, og=None, og_block=0):
    T, D = h.shape
    K = w.shape[0]
    per_b = S // tm
    gated = og is not None
    xs = (x, og) if gated else (x,)
    in_specs = [pl.BlockSpec((tm, K), lambda i: (i, 0))]
    if gated:
        in_specs.append(pl.BlockSpec((tm, K), lambda i: (i, og_block)))
    in_specs += [
        pl.BlockSpec((K, D), lambda i: (0, 0)),
        pl.BlockSpec((tm, D), lambda i: (i, 0)),
        pl.BlockSpec((None, 1, D), lambda i: (i // per_b, 0, 0)),
    ]
    return pl.pallas_call(
        functools.partial(_proj_res_kernel, gated=gated),
        grid=(T // tm,),
        in_specs=in_specs,
        out_specs=pl.BlockSpec((tm, D), lambda i: (i, 0)),
        out_shape=jax.ShapeDtypeStruct((T, D), F32),
        compiler_params=_params("parallel"),
        name="proj_residual",
    )(*xs, w, h, gate)


def _ffn_dense_kernel(h_ref, g_ref, sh_ref, sc_ref, gt_ref, wg_ref, wu_ref, wd_ref, o_ref, a_scr,
                      acc_scr):
    f = pl.program_id(1)

    @pl.when(f == 0)
    def _():
        a_scr[...] = _norm_mod(h_ref[...], g_ref[...], sh_ref[...], sc_ref[...]).astype(BF16)
        acc_scr[...] = jnp.zeros_like(acc_scr)

    a = a_scr[...]
    gate = jnp.dot(a, wg_ref[...], preferred_element_type=F32)
    up = jnp.dot(a, wu_ref[...], preferred_element_type=F32)
    act = (_silu(gate) * up).astype(BF16)
    acc_scr[...] += jnp.dot(act, wd_ref[...], preferred_element_type=F32)

    @pl.when(f == pl.num_programs(1) - 1)
    def _():
        o_ref[...] = h_ref[...] + gt_ref[...] * acc_scr[...]


def _ffn_dense(h, g, sh, sc, gate, wg, wu, wd, S, tm, tf):
    T, D = h.shape
    F = wg.shape[1]
    per_b = S // tm
    return pl.pallas_call(
        _ffn_dense_kernel,
        grid=(T // tm, F // tf),
        in_specs=[
            pl.BlockSpec((tm, D), lambda i, f: (i, 0)),
            pl.BlockSpec((1, D), lambda i, f: (0, 0)),
            pl.BlockSpec((None, 1, D), lambda i, f: (i // per_b, 0, 0)),
            pl.BlockSpec((None, 1, D), lambda i, f: (i // per_b, 0, 0)),
            pl.BlockSpec((None, 1, D), lambda i, f: (i // per_b, 0, 0)),
            pl.BlockSpec((D, tf), lambda i, f: (0, f)),
            pl.BlockSpec((D, tf), lambda i, f: (0, f)),
            pl.BlockSpec((tf, D), lambda i, f: (f, 0)),
        ],
        out_specs=pl.BlockSpec((tm, D), lambda i, f: (i, 0)),
        out_shape=jax.ShapeDtypeStruct((T, D), F32),
        scratch_shapes=[pltpu.VMEM((tm, D), BF16), pltpu.VMEM((tm, D), F32)],
        compiler_params=_params("parallel", "arbitrary"),
        name="ffn_dense",
    )(h, g, sh, sc, gate, wg, wu, wd)


def _ffn_moe_kernel(ut_ref, ue_ref, ulo_ref, uhi_ref, ufirst_ref, uvalid_ref, x_ref, wg_ref, wu_ref,
                    wd_ref, o_ref, acc_scr):
    u = pl.program_id(0)
    f = pl.program_id(1)
    valid = uvalid_ref[u] > 0

    @pl.when(jnp.logical_and(jnp.logical_and(valid, ufirst_ref[u] > 0), f == 0))
    def _():
        acc_scr[...] = jnp.zeros_like(acc_scr)

    @pl.when(valid)
    def _():
        x = x_ref[...]
        gate = jnp.dot(x, wg_ref[...], preferred_element_type=F32)
        up = jnp.dot(x, wu_ref[...], preferred_element_type=F32)
        row = lax.broadcasted_iota(jnp.int32, gate.shape, 0)
        mine = jnp.logical_and(row >= ulo_ref[u], row < uhi_ref[u])
        act = jnp.where(mine, _silu(gate) * up, 0.0).astype(BF16)
        acc_scr[...] += jnp.dot(act, wd_ref[...], preferred_element_type=F32)

    @pl.when(jnp.logical_and(valid, f == pl.num_programs(1) - 1))
    def _():
        o_ref[...] = acc_scr[...]


def _ffn_moe(units, xs, wg, wu, wd, tm, tf):
    N, D = xs.shape
    E, _, F = wg.shape
    nf = F // tf
    n_units = units[0].shape[0]

    def xrow(u, f, ut, ue, ulo, uhi, ufirst, uvalid):
        return (ut[u], 0)

    def wcol(u, f, ut, ue, ulo, uhi, ufirst, uvalid):
        return (ue[u], 0, jnp.where(uvalid[u] > 0, f, nf - 1))

    def wrow(u, f, ut, ue, ulo, uhi, ufirst, uvalid):
        return (ue[u], jnp.where(uvalid[u] > 0, f, nf - 1), 0)

    grid_spec = pltpu.PrefetchScalarGridSpec(
        num_scalar_prefetch=6,
        grid=(n_units, nf),
        in_specs=[
            pl.BlockSpec((tm, D), xrow),
            pl.BlockSpec((None, D, tf), wcol),
            pl.BlockSpec((None, D, tf), wcol),
            pl.BlockSpec((None, tf, D), wrow),
        ],
        out_specs=pl.BlockSpec((tm, D), xrow),
        scratch_shapes=[pltpu.VMEM((tm, D), F32)],
    )
    return pl.pallas_call(
        _ffn_moe_kernel,
        grid_spec=grid_spec,
        out_shape=jax.ShapeDtypeStruct((N, D), F32),
        compiler_params=_params("arbitrary", "arbitrary"),
        name="ffn_moe",
    )(*units, xs, wg, wu, wd)


def _combine_kernel(h_ref, y_ref, wt_ref, gt_ref, o_ref):
    D = h_ref.shape[1]
    wt = wt_ref[...]
    ff = wt[:, 0:1] * y_ref[:, 0:D] + wt[:, 1:2] * y_ref[:, D:2 * D]
    o_ref[...] = h_ref[...] + gt_ref[...] * ff


def _moe_combine(h, y2, wt8, gate, S, tm):
    T, D = h.shape
    per_b = S // tm
    return pl.pallas_call(
        _combine_kernel,
        grid=(T // tm,),
        in_specs=[
            pl.BlockSpec((tm, D), lambda i: (i, 0)),
            pl.BlockSpec((tm, 2 * D), lambda i: (i, 0)),
            pl.BlockSpec((tm, 8), lambda i: (i, 0)),
            pl.BlockSpec((None, 1, D), lambda i: (i // per_b, 0, 0)),
        ],
        out_specs=pl.BlockSpec((tm, D), lambda i: (i, 0)),
        out_shape=jax.ShapeDtypeStruct((T, D), F32),
        compiler_params=_params("parallel"),
        name="moe_combine",
    )(h, y2, wt8, gate)


def _router_kernel(h_ref, g_ref, sh_ref, sc_ref, rw_ref, rb_ref, m_ref, idx_ref, wt_ref, *, n_e):
    a = _norm_mod(h_ref[...], g_ref[...], sh_ref[...], sc_ref[...]).astype(BF16)
    m_ref[...] = a
    logits = jnp.dot(a, rw_ref[...], preferred_element_type=F32) + rb_ref[...]
    lane = lax.broadcasted_iota(jnp.int32, logits.shape, 1)
    lane_f = lane.astype(F32)
    logits = jnp.where(lane < n_e, logits, NEG_BIG)
    m1 = jnp.max(logits, axis=-1, keepdims=True)
    i1 = jnp.min(jnp.where(logits == m1, lane_f, float(LANES)), axis=-1, keepdims=True)
    rest = jnp.where(lane_f == i1, NEG_BIG, logits)
    m2 = jnp.max(rest, axis=-1, keepdims=True)
    i2 = jnp.min(jnp.where(rest == m2, lane_f, float(LANES)), axis=-1, keepdims=True)
    e2 = jnp.exp(m2 - m1)
    w1 = 1.0 / (1.0 + e2)
    w2 = e2 * w1
    w = jnp.where(lane == 0, w1, jnp.where(lane == 1, w2, 0.0))
    ii = jnp.where(lane == 0, i1, jnp.where(lane == 1, i2, 0.0)).astype(jnp.int32)
    wt_ref[...] = w[:, :8]
    idx_ref[...] = ii[:, :8]


def _router(h, g, sh, sc, rw, rb, S, tm, n_e):
    T, D = h.shape
    per_b = S // tm
    return pl.pallas_call(
        functools.partial(_router_kernel, n_e=n_e),
        grid=(T // tm,),
        in_specs=[
            pl.BlockSpec((tm, D), lambda i: (i, 0)),
            pl.BlockSpec((1, D), lambda i: (0, 0)),
            pl.BlockSpec((None, 1, D), lambda i: (i // per_b, 0, 0)),
            pl.BlockSpec((None, 1, D), lambda i: (i // per_b, 0, 0)),
            pl.BlockSpec((D, LANES), lambda i: (0, 0)),
            pl.BlockSpec((1, LANES), lambda i: (0, 0)),
        ],
        out_specs=[
            pl.BlockSpec((tm, D), lambda i: (i, 0)),
            pl.BlockSpec((tm, 8), lambda i: (i, 0)),
            pl.BlockSpec((tm, 8), lambda i: (i, 0)),
        ],
        out_shape=[
            jax.ShapeDtypeStruct((T, D), BF16),
            jax.ShapeDtypeStruct((T, 8), jnp.int32),
            jax.ShapeDtypeStruct((T, 8), F32),
        ],
        compiler_params=_params("parallel"),
        name="router",
    )(h, g, sh, sc, rw, rb)


def _forget_kernel(h_ref, g_ref, sh_ref, sc_ref, wf_ref, bf_ref, o_ref, carry_ref, *, per_b):
    @pl.when(pl.program_id(0) % per_b == 0)
    def _():
        carry_ref[...] = jnp.zeros_like(carry_ref)

    a = _norm_mod(h_ref[...], g_ref[...], sh_ref[...], sc_ref[...]).astype(BF16)
    z = jnp.dot(a, wf_ref[...], preferred_element_type=F32) + bf_ref[...]
    lf = jnp.minimum(z, 0.0) - jnp.log(1.0 + jnp.exp(-jnp.abs(z)))
    tm = lf.shape[0]
    r = lax.broadcasted_iota(jnp.int32, (tm, tm), 0)
    c = lax.broadcasted_iota(jnp.int32, (tm, tm), 1)
    tri = jnp.where(c <= r, 1.0, 0.0).astype(F32)
    cs = jnp.dot(tri, lf, preferred_element_type=F32, precision=lax.Precision.HIGHEST)
    out = cs + carry_ref[...]
    carry_ref[...] = out[tm - 1:tm, :]
    out2 = out * LOG2E
    hi = out2.astype(BF16)
    r1 = out2 - hi.astype(F32)
    mid = r1.astype(BF16)
    lo = (r1 - mid.astype(F32)).astype(BF16)
    o_ref[:, 0:LANES] = hi
    o_ref[:, LANES:2 * LANES] = mid
    o_ref[:, 2 * LANES:3 * LANES] = lo


def _forget_cumsum(h, g, sh, sc, wf, bf, S, tm):
    T, D = h.shape
    per_b = S // tm
    return pl.pallas_call(
        functools.partial(_forget_kernel, per_b=per_b),
        grid=(T // tm,),
        in_specs=[
            pl.BlockSpec((tm, D), lambda i: (i, 0)),
            pl.BlockSpec((1, D), lambda i: (0, 0)),
            pl.BlockSpec((None, 1, D), lambda i: (i // per_b, 0, 0)),
            pl.BlockSpec((None, 1, D), lambda i: (i // per_b, 0, 0)),
            pl.BlockSpec((D, LANES), lambda i: (0, 0)),
            pl.BlockSpec((1, LANES), lambda i: (0, 0)),
        ],
        out_specs=pl.BlockSpec((tm, 3 * LANES), lambda i: (i, 0)),
        out_shape=jax.ShapeDtypeStruct((T, 3 * LANES), BF16),
        scratch_shapes=[pltpu.VMEM((1, LANES), F32)],
        compiler_params=_params("arbitrary"),
        name="forget_cumsum",
    )(h, g, sh, sc, wf, bf)


def _fox_prep_kernel(kp_ref, vp_ref, f_ref, pm_ref, ka_ref, vt_ref, kn_ref, *, dh):
    c = pl.program_id(2)
    tk = kp_ref.shape[0]
    kp = kp_ref[...]
    lane = lax.broadcasted_iota(jnp.int32, (tk, LANES), 1)
    ex = jnp.dot(f_ref[...], pm_ref[...], preferred_element_type=F32)
    ex = jnp.where((lane % dh) < 3, 1.0, ex).astype(BF16)
    ka_ref[0] = jnp.where(lane < dh, kp, ex)
    ka_ref[1] = jnp.where(lane < dh, ex, kp)
    vt = vp_ref[...].astype(F32).T
    row16 = lax.broadcasted_iota(jnp.int32, (DV_ROWS - dh, tk), 0)
    tail = jnp.where(row16 == 0, 1.0, 0.0).astype(BF16)
    vt_ref[0, 0:dh, :] = vt[0:dh].astype(BF16)
    vt_ref[1, 0:dh, :] = vt[dh:2 * dh].astype(BF16)
    vt_ref[0, dh:DV_ROWS, :] = tail
    vt_ref[1, dh:DV_ROWS, :] = tail
    k2 = kp.astype(F32)
    k2 = k2 * k2
    na = jnp.max(jnp.sum(jnp.where(lane < dh, k2, 0.0), axis=1, keepdims=True), axis=0, keepdims=True)
    nb = jnp.max(jnp.sum(jnp.where(lane < dh, 0.0, k2), axis=1, keepdims=True), axis=0, keepdims=True)
    lane8 = lax.broadcasted_iota(jnp.int32, (8, LANES), 1)
    row8 = lax.broadcasted_iota(jnp.int32, (8, LANES), 0)

    @pl.when(c == 0)
    def _():
        kn_ref[...] = jnp.zeros_like(kn_ref)

    upd = jnp.where(row8 == 0, na, jnp.where(row8 == 1, nb, 0.0))
    kn_ref[...] = jnp.where(lane8 == c, upd, kn_ref[...])


def _fox_prep(kv, fparts, pm, B, S, D, dh, tk):
    H = D // dh
    nb = S // tk
    assert nb <= LANES
    return pl.pallas_call(
        functools.partial(_fox_prep_kernel, dh=dh),
        grid=(B, H // 2, nb),
        in_specs=[
            pl.BlockSpec((tk, LANES), lambda b, p, c: (b * nb + c, p)),
            pl.BlockSpec((tk, LANES), lambda b, p, c: (b * nb + c, D // LANES + p)),
            pl.BlockSpec((tk, 3 * LANES), lambda b, p, c: (b * nb + c, 0)),
            pl.BlockSpec((None, 3 * LANES, LANES), lambda b, p, c: (p, 0, 0)),
        ],
        out_specs=[
            pl.BlockSpec((None, 2, tk, LANES), lambda b, p, c: (b, p, c, 0)),
            pl.BlockSpec((None, 2, DV_ROWS, tk), lambda b, p, c: (b, p, 0, c)),
            pl.BlockSpec((None, None, 8, LANES), lambda b, p, c: (b, p, 0, 0)),
        ],
        out_shape=[
            jax.ShapeDtypeStruct((B, H, S, LANES), BF16),
            jax.ShapeDtypeStruct((B, H, DV_ROWS, S), BF16),
            jax.ShapeDtypeStruct((B, H // 2, 8, LANES), F32),
        ],
        compiler_params=_params("parallel", "parallel", "arbitrary"),
        name="fox_prep",
    )(kv, kv, fparts, pm)


def _fox_kernel(q_ref, gq_ref, gb_ref, kn_ref, ka_ref, vt_ref, o_ref, sa0, sa1, sb0, sb1, *,
                tq, tk, dh):
    i = pl.program_id(2)
    qT = q_ref[...].T
    g16 = gq_ref[...]
    row8 = lax.broadcasted_iota(jnp.int32, (8, tq), 0)
    xa = jnp.where(row8 < 3, g16[0:8], jnp.where(row8 < 6, 1.0, 0.0))
    xb = jnp.where(row8 < 3, g16[8:16], jnp.where(row8 < 6, 1.0, 0.0))
    zpad = jnp.zeros((LANES - dh - 8, tq), F32)
    qa = jnp.concatenate([qT[0:dh], xa, zpad], axis=0).astype(BF16)
    qb = jnp.concatenate([xb, zpad, qT[dh:2 * dh]], axis=0).astype(BF16)

    q2 = qT * qT
    qna = jnp.max(jnp.sum(q2[0:dh], axis=0, keepdims=True), axis=1, keepdims=True)
    qnb = jnp.max(jnp.sum(q2[dh:2 * dh], axis=0, keepdims=True), axis=1, keepdims=True)
    kn = kn_ref[...]
    gb = gb_ref[...]
    lane = lax.broadcasted_iota(jnp.int32, (1, LANES), 1)
    at_i = lane == i

    def pick(row):
        return jnp.sum(jnp.where(at_i, row, 0.0), axis=1, keepdims=True)

    def needed(qn, knr, glast, gfirst):
        bound = NORM_SLACK * jnp.sqrt(qn) * (jnp.sqrt(knr) + jnp.sqrt(pick(knr))) + pick(gfirst) - glast
        return bound > -SKIP_LOG2

    need = jnp.logical_or(needed(qna, kn[0:1], gb[0:1], gb[2:3]), needed(qnb, kn[1:2], gb[1:2], gb[3:4]))
    lane_f = lane.astype(F32)
    i_f = i.astype(F32)
    cand = jnp.where(jnp.logical_and(need, lane < i), lane_f, i_f)
    jmin = jnp.min(cand).astype(jnp.int32)

    def scores(h, q, blk):
        k0 = pl.multiple_of(blk * tk, tk)
        return jnp.dot(ka_ref[h, pl.ds(k0, tk), :], q, preferred_element_type=F32)

    def update(s_ref, h, blk, m, acc):
        k0 = pl.multiple_of(blk * tk, tk)
        s = s_ref[...]
        m_new = jnp.maximum(m, jnp.max(s, axis=0, keepdims=True))
        alpha = jnp.exp2(m - m_new)
        p = jnp.exp2(s - m_new).astype(BF16)
        pv = jnp.dot(vt_ref[h, :, pl.ds(k0, tk)], p, preferred_element_type=F32)
        return m_new, alpha * acc + pv

    n_off = i - jmin
    n_pairs = (n_off + 2) // 2
    key = lax.broadcasted_iota(jnp.int32, (tk, tq), 0)
    qry = lax.broadcasted_iota(jnp.int32, (tk, tq), 1)
    sa0[...] = jnp.where(key <= qry, scores(0, qa, i), NEG_BIG)
    sb0[...] = jnp.where(key <= qry, scores(1, qb, i), NEG_BIG)

    def pair(u, carry):
        ma, acca, mb, accb = carry
        b1 = jmin + 2 * u
        b0 = jnp.where(u == 0, i, b1 - 1)
        sa1[...] = scores(0, qa, b1)
        sb1[...] = scores(1, qb, b1)
        ma, acca = update(sa0, 0, b0, ma, acca)
        mb, accb = update(sb0, 1, b0, mb, accb)
        sa0[...] = scores(0, qa, b1 + 1)
        sb0[...] = scores(1, qb, b1 + 1)
        ma, acca = update(sa1, 0, b1, ma, acca)
        mb, accb = update(sb1, 1, b1, mb, accb)
        return ma, acca, mb, accb

    m0 = jnp.full((1, tq), NEG_BIG, F32)
    a0 = jnp.zeros((DV_ROWS, tq), F32)
    ma, acca, mb, accb = lax.fori_loop(0, n_pairs - 1, pair, (m0, a0, m0, a0))
    u = n_pairs - 1
    real = 2 * u < n_off
    b1 = jnp.where(real, jmin + 2 * u, i)
    b0 = jnp.where(u == 0, i, jmin + 2 * u - 1)
    sa1[...] = jnp.where(real, scores(0, qa, b1), NEG_BIG)
    sb1[...] = jnp.where(real, scores(1, qb, b1), NEG_BIG)
    ma, acca = update(sa0, 0, b0, ma, acca)
    mb, accb = update(sb0, 1, b0, mb, accb)
    ma, acca = update(sa1, 0, b1, ma, acca)
    mb, accb = update(sb1, 1, b1, mb, accb)
    oa = acca[:dh, :] * (1.0 / acca[dh:dh + 1, :])
    ob = accb[:dh, :] * (1.0 / accb[dh:dh + 1, :])
    o_ref[...] = jnp.concatenate([oa, ob], axis=0).T


def _fox_attention(qg, gq, gb, kn, ka, vt, B, S, D, dh, tq):
    T = B * S
    H = D // dh
    nq = S // tq
    return pl.pallas_call(
        functools.partial(_fox_kernel, tq=tq, tk=tq, dh=dh),
        grid=(B, H // 2, nq),
        in_specs=[
            pl.BlockSpec((tq, LANES), lambda b, p, i: (b * nq + i, p)),
            pl.BlockSpec((None, None, 16, tq), lambda b, p, i: (b, p, 0, i)),
            pl.BlockSpec((None, None, 8, LANES), lambda b, p, i: (b, p, 0, 0)),
            pl.BlockSpec((None, None, 8, LANES), lambda b, p, i: (b, p, 0, 0)),
            pl.BlockSpec((None, 2, S, LANES), lambda b, p, i: (b, p, 0, 0)),
            pl.BlockSpec((None, 2, DV_ROWS, S), lambda b, p, i: (b, p, 0, 0)),
        ],
        out_specs=pl.BlockSpec((tq, LANES), lambda b, p, i: (b * nq + i, p)),
        out_shape=jax.ShapeDtypeStruct((T, D), F32),
        scratch_shapes=[pltpu.VMEM((tq, tq), F32) for _ in range(4)],
        compiler_params=_params("parallel", "parallel", "arbitrary"),
        name="fox_attention",
    )(qg, gq, gb, kn, ka, vt)


def _fox_gate_tables(fparts, B, S, H, tk):
    f3 = fparts.reshape(B, S, 3, LANES)[:, :, :, :H].astype(F32)
    gq = f3.reshape(B, S, 3, H // 2, 2).transpose(0, 3, 4, 2, 1)
    gq = jnp.pad(gq, ((0, 0), (0, 0), (0, 0), (0, 5), (0, 0))).reshape(B, H // 2, 16, S)
    G = jnp.sum(f3, axis=2)
    nb = S // tk
    Gb = G.reshape(B, nb, tk, H // 2, 2)
    last = Gb[:, :, tk - 1].transpose(0, 2, 3, 1)
    first = Gb[:, :, 0].transpose(0, 2, 3, 1)
    gb = jnp.concatenate([last, first, jnp.zeros_like(last), jnp.zeros_like(last)], axis=2)
    gb = jnp.pad(gb, ((0, 0), (0, 0), (0, 0), (0, LANES - nb)))
    return gq, gb


def _fox_placement(H, dh):
    p = jnp.arange(H // 2)[:, None, None]
    r = jnp.arange(3 * LANES)[None, :, None]
    l = jnp.arange(LANES)[None, None, :]
    j = r // LANES
    head = r % LANES
    hit_b = jnp.logical_and(head == 2 * p + 1, l == 3 + j)
    hit_a = jnp.logical_and(head == 2 * p, l == dh + 3 + j)
    return jnp.where(jnp.logical_or(hit_a, hit_b), -1.0, 0.0).astype(BF16)


def _final_kernel(h_ref, g_ref, sh_ref, sc_ref, o_ref):
    o_ref[...] = _norm_mod(h_ref[...], g_ref[...], sh_ref[...], sc_ref[...])


def _final_norm(h, g, sh, sc, S, tm):
    T, D = h.shape
    per_b = S // tm
    return pl.pallas_call(
        _final_kernel,
        grid=(T // tm,),
        in_specs=[
            pl.BlockSpec((tm, D), lambda i: (i, 0)),
            pl.BlockSpec((1, D), lambda i: (0, 0)),
            pl.BlockSpec((None, 1, D), lambda i: (i // per_b, 0, 0)),
            pl.BlockSpec((None, 1, D), lambda i: (i // per_b, 0, 0)),
        ],
        out_specs=pl.BlockSpec((tm, D), lambda i: (i, 0)),
        out_shape=jax.ShapeDtypeStruct((T, D), F32),
        compiler_params=_params("parallel"),
        name="final_norm",
    )(h, g, sh, sc)


def _dispatch_tables(top_idx, n_e, tm):
    T, K = top_idx.shape
    n = T * K
    n_tiles = n // tm
    n_units = n_tiles + n_e
    e_flat = top_idx.reshape(n)
    order = jnp.argsort(e_flat, stable=True).astype(jnp.int32)
    pos = jnp.argsort(order).astype(jnp.int32)
    ids = jnp.arange(n_e, dtype=jnp.int32)
    counts = jnp.sum(e_flat[:, None] == ids[None, :], axis=0).astype(jnp.int32)
    ends = jnp.cumsum(counts)
    starts = ends - counts
    t0 = jnp.arange(n_tiles, dtype=jnp.int32)[:, None] * tm
    present = jnp.logical_and(starts[None, :] < t0 + tm, ends[None, :] > t0)
    present = jnp.logical_and(present, counts[None, :] > 0)
    flat = jnp.arange(n_tiles * n_e, dtype=jnp.int32)
    keys = jnp.sort(jnp.where(present.reshape(-1), flat, n_tiles * n_e))[:n_units]
    valid = keys < n_tiles * n_e
    n_valid = jnp.sum(valid.astype(jnp.int32))
    keys = jnp.where(valid, keys, keys[jnp.maximum(n_valid - 1, 0)])
    ut = keys // n_e
    ue = keys % n_e
    ulo = jnp.clip(starts[ue] - ut * tm, 0, tm)
    uhi = jnp.clip(ends[ue] - ut * tm, 0, tm)
    prev_t = jnp.concatenate([jnp.full((1,), -1, jnp.int32), ut[:-1]])
    ufirst = (ut != prev_t).astype(jnp.int32)
    units = tuple(a.astype(jnp.int32) for a in (ut, ue, ulo, uhi, ufirst, valid))
    return order, pos.reshape(T, K), units


def _tile(S, want):
    t = min(S, want)
    assert S % t == 0
    return t


def kernel(x, c, ada_w, ada_b, norm_g, ret_w_in, ret_w_o, kv_ada_w, kv_ada_b, kv_norm_g, fox_w_kv, fox_w_f, fox_b_f, fox_w_qg, fox_w_o, ffn_w_gate, ffn_w_up, ffn_w_down, router_w, router_b, moe_w_gate, moe_w_up, moe_w_down, final_ada_w, final_ada_b, final_norm_g):
    B, S, D = x.shape
    T = B * S
    depth = ada_w.shape[0]
    n_a = ret_w_in.shape[0]
    n_e = router_w.shape[-1]
    dh = D // FOX_HEADS
    dk = D // RET_HEADS
    assert dk == 2 * LANES and S % CHUNK == 0 and n_e <= 8

    tm = _tile(S, 512)
    tm_ffn = _tile(S, 1024)
    tm_moe = _tile(S, 1024)
    tq = _tile(S, 512)
    lc = _tile(S, 512)
    f_dense = ffn_w_gate.shape[-1]
    f_moe = moe_w_gate.shape[-1]
    tf_dense = f_dense // 2 if (f_dense // 2) % LANES == 0 else f_dense
    tf_moe = 512 if f_moe % 512 == 0 else f_moe

    c_pad = jnp.zeros((8, D), F32).at[:B].set(c)
    ada = _ada(c_pad, ada_w, ada_b[:, None, :])[:, :B]
    extra_w = jnp.stack([kv_ada_w, final_ada_w])
    extra_b = jnp.stack([kv_ada_b, final_ada_b])[:, None, :]
    extra = _ada(c_pad, extra_w, extra_b)[:, :B]
    vec = lambda a: a[:, None, :]

    h = x.reshape(T, D)
    cos, sin = _rotary_tables(S, dk)
    ret_tabs = _retention_tables()
    shared = None

    for l in range(depth):
        sh1, sc1, g1, sh2, sc2, g2 = [vec(a) for a in jnp.split(ada[l], 6, axis=-1)]
        ng1 = norm_g[l, 0][None, :]
        ng2 = norm_g[l, 1][None, :]
        if l == n_a:
            kv_sh, kv_sc = [vec(a) for a in jnp.split(extra[0], 2, axis=-1)]
            kvg = kv_norm_g[None, :]
            kv = _norm_mod_matmul(h, kvg, kv_sh, kv_sc, fox_w_kv.astype(BF16), S,
                                  modes=("plain",) * 4, scales=(1.0,) * 4, cw=2 * D // 4,
                                  out_dtype=BF16, tm=tm)
            wf = jnp.zeros((D, LANES), F32).at[:, :FOX_HEADS].set(fox_w_f).astype(BF16)
            bf = jnp.zeros((1, LANES), F32).at[0, :FOX_HEADS].set(fox_b_f)
            fparts = _forget_cumsum(h, kvg, kv_sh, kv_sc, wf, bf, S, tm)
            ka, vt, kn = _fox_prep(kv, fparts, _fox_placement(FOX_HEADS, dh), B, S, D, dh, tq)
            gq, gb = _fox_gate_tables(fparts, B, S, FOX_HEADS, tq)
            shared = (gq, gb, kn, ka, vt)
        if l < n_a:
            w_in = ret_w_in[l].astype(BF16)
            qkv = _norm_mod_matmul(
                h, ng1, sh1, sc1, w_in[:, :4 * D], S,
                modes=("rot",) * 4 + ("plain",) * 4,
                scales=(1.0,) * 2 + (dk ** -0.5,) * 2 + (1.0,) * 4,
                cw=D // 2, out_dtype=BF16, tm=tm, rot=(cos, sin))
            gate = _norm_mod_matmul(h, ng1, sh1, sc1, w_in[:, 4 * D:], S,
                                    modes=("plain",) * 4, scales=(1.0,) * 4, cw=D // 2,
                                    out_dtype=F32, tm=tm)
            y = _retention(qkv, gate, ret_tabs, B, S, lc)
            h = _proj_residual(y, ret_w_o[l].astype(BF16), h, g1, S, tm)
        else:
            j = l - n_a
            qg = _norm_mod_matmul(h, ng1, sh1, sc1, fox_w_qg[j].astype(BF16), S,
                                  modes=("plain",) * 4, scales=(dh ** -0.5 * LOG2E,) * 2 + (1.0,) * 2,
                                  cw=D // 2, out_dtype=F32, tm=tm)
            o = _fox_attention(qg, *shared, B, S, D, dh, tq)
            h = _proj_residual(o, fox_w_o[j].astype(BF16), h, g1, S, tm, og=qg, og_block=1)
        i = l // 2
        if l % 2 == 0:
            h = _ffn_dense(h, ng2, sh2, sc2, g2, ffn_w_gate[i].astype(BF16), ffn_w_up[i].astype(BF16),
                           ffn_w_down[i].astype(BF16), S, tm_ffn, tf_dense)
        else:
            rw = jnp.zeros((D, LANES), F32).at[:, :n_e].set(router_w[i]).astype(BF16)
            rb = jnp.zeros((1, LANES), F32).at[0, :n_e].set(router_b[i])
            m, idx8, wt8 = _router(h, ng2, sh2, sc2, rw, rb, S, tm, n_e)
            order, pos, units = _dispatch_tables(idx8[:, :TOP_K], n_e, tm_moe)
            xs = jnp.take(m, order // TOP_K, axis=0)
            ys = _ffn_moe(units, xs, moe_w_gate[i].astype(BF16), moe_w_up[i].astype(BF16),
                          moe_w_down[i].astype(BF16), tm_moe, tf_moe)
            y2 = jnp.take(ys, pos.reshape(-1), axis=0).reshape(T, TOP_K * D)
            h = _moe_combine(h, y2, wt8, g2, S, tm)

    f_sh, f_sc = [vec(a) for a in jnp.split(extra[1], 2, axis=-1)]
    out = _final_norm(h, final_norm_g[None, :], f_sh, f_sc, S, tm)
    return out.reshape(B, S, D)
```

```python
import functools
import math

import jax
import jax.numpy as jnp
from jax import lax
from jax.experimental import pallas as pl
from jax.experimental.pallas import tpu as pltpu

F32 = jnp.float32
BF16 = jnp.bfloat16

CHUNK = 64
RET_HEADS = 4
FOX_HEADS = 16
ROPE_BASE = 10000.0
TOP_K = 2
EPS = 1e-6

LANES = 128
VMEM_LIMIT_BYTES = 56 * 1024 * 1024
NEG_BIG = -1e30
DV_ROWS = 80
SKIP_LOG2 = 160.0
NORM_SLACK = 1.02
LOG2E = math.log2(math.e)


def _params(*sem):
    return pltpu.CompilerParams(dimension_semantics=sem, vmem_limit_bytes=VMEM_LIMIT_BYTES)


def _silu(x):
    return x * (1.0 / (1.0 + jnp.exp(-x)))


def _sigmoid(x):
    return 1.0 / (1.0 + jnp.exp(-x))


def _norm_mod(x, g, sh, sc):
    var = jnp.mean(x * x, axis=-1, keepdims=True)
    y = (x * lax.rsqrt(var + EPS)) * g
    return y * (1.0 + sc) + sh


def _ada_kernel(c_ref, w_ref, b_ref, o_ref):
    ca = _silu(c_ref[...]).astype(BF16)
    o_ref[...] = jnp.dot(ca, w_ref[...].astype(BF16), preferred_element_type=F32) + b_ref[...]


def _ada(c_pad, w, b, tn=1024):
    L, D, N = w.shape
    return pl.pallas_call(
        _ada_kernel,
        grid=(L, N // tn),
        in_specs=[
            pl.BlockSpec((8, D), lambda l, j: (0, 0)),
            pl.BlockSpec((None, D, tn), lambda l, j: (l, 0, j)),
            pl.BlockSpec((None, 1, tn), lambda l, j: (l, 0, j)),
        ],
        out_specs=pl.BlockSpec((None, 8, tn), lambda l, j: (l, 0, j)),
        out_shape=jax.ShapeDtypeStruct((L, 8, N), F32),
        compiler_params=_params("parallel", "parallel"),
        name="ada_proj",
    )(c_pad, w, b)


def _nmm_kernel(*refs, n_chunks, cw, modes, scales, out_dtype):
    h_ref, g_ref, sh_ref, sc_ref, w_ref = refs[:5]
    has_rot = any(m == "rot" for m in modes)
    if has_rot:
        cos_ref, sin_ref, o_ref = refs[5:8]
    else:
        o_ref = refs[5]
    a = _norm_mod(h_ref[...], g_ref[...], sh_ref[...], sc_ref[...]).astype(BF16)
    for c in range(n_chunks):
        acc = jnp.dot(a, w_ref[:, c * cw:(c + 1) * cw], preferred_element_type=F32)
        if modes[c] == "rot":
            cos = cos_ref[...]
            sin = sin_ref[...]
            parts = []
            for hh in range(cw // (2 * LANES)):
                x1 = acc[:, (2 * hh) * LANES:(2 * hh + 1) * LANES]
                x2 = acc[:, (2 * hh + 1) * LANES:(2 * hh + 2) * LANES]
                parts.append(x1 * cos - x2 * sin)
                parts.append(x2 * cos + x1 * sin)
            acc = jnp.concatenate(parts, axis=-1)
        if scales[c] != 1.0:
            acc = acc * scales[c]
        o_ref[:, c * cw:(c + 1) * cw] = acc.astype(out_dtype)


def _norm_mod_matmul(h, g, sh, sc, w, S, *, modes, scales, cw, out_dtype, tm, rot=None, w_layer=0,
                     w_col=0):
    T, D = h.shape
    n_chunks = len(modes)
    N = n_chunks * cw
    per_b = S // tm
    in_specs = [
        pl.BlockSpec((tm, D), lambda i: (i, 0)),
        pl.BlockSpec((1, D), lambda i: (0, 0)),
        pl.BlockSpec((None, 1, D), lambda i: (i // per_b, 0, 0)),
        pl.BlockSpec((None, 1, D), lambda i: (i // per_b, 0, 0)),
        pl.BlockSpec((None, D, N), lambda i: (w_layer, 0, w_col)),
    ]
    args = [h, g, sh, sc, w]
    if rot is not None:
        in_specs += [pl.BlockSpec((tm, LANES), lambda i: (i % per_b, 0))] * 2
        args += list(rot)
    return pl.pallas_call(
        functools.partial(_nmm_kernel, n_chunks=n_chunks, cw=cw, modes=modes, scales=scales,
                          out_dtype=out_dtype),
        grid=(T // tm,),
        in_specs=in_specs,
        out_specs=pl.BlockSpec((tm, N), lambda i: (i, 0)),
        out_shape=jax.ShapeDtypeStruct((T, N), out_dtype),
        compiler_params=_params("parallel"),
        name="norm_mod_matmul",
    )(*args)


def _ret_kernel(q_ref, k_ref, v_ref, g_ref, intra_ref, qd_ref, kd_ref, cd_ref, o_ref, st_ref, *,
                n_sub, dk, dv):
    @pl.when(pl.program_id(1) == 0)
    def _():
        st_ref[...] = jnp.zeros_like(st_ref)

    def body(c, carry):
        r0 = pl.multiple_of(c * CHUNK, CHUNK)
        rows = pl.ds(r0, CHUNK)
        for h in range(RET_HEADS):
            qc = q_ref[rows, h * dk:(h + 1) * dk]
            kc = k_ref[rows, h * dk:(h + 1) * dk]
            vc = v_ref[rows, h * dv:(h + 1) * dv]
            state = st_ref[h]
            scores = lax.dot_general(qc, kc, (((1,), (1,)), ((), ())),
                                     preferred_element_type=F32) * intra_ref[h]
            o = (jnp.dot(scores.astype(BF16), vc, preferred_element_type=F32)
                 + jnp.dot(qc, state.astype(BF16), preferred_element_type=F32) * qd_ref[h])
            kdk = (kc.astype(F32) * kd_ref[h]).astype(BF16)
            st_ref[h] = state * cd_ref[h] + lax.dot_general(kdk, vc, (((0,), (0,)), ((), ())),
                                                             preferred_element_type=F32)
            mu = jnp.mean(o, axis=-1, keepdims=True)
            d = o - mu
            var = jnp.mean(d * d, axis=-1, keepdims=True)
            y = d * lax.rsqrt(var + EPS)
            gate = g_ref[rows, h * dv:(h + 1) * dv]
            o_ref[rows, h * dv:(h + 1) * dv] = (_silu(gate) * y).astype(o_ref.dtype)
        return carry

    lax.fori_loop(0, n_sub, body, 0)


def _retention(qkv, gate, tabs, B, S, lc):
    T = qkv.shape[0]
    D = qkv.shape[1] // 4
    dk = D // RET_HEADS
    dv = 2 * D // RET_HEADS
    H = RET_HEADS
    per_b = S // lc
    intra, qd, kd, cd = tabs
    row = lambda b, s: b * per_b + s
    return pl.pallas_call(
        functools.partial(_ret_kernel, n_sub=lc // CHUNK, dk=dk, dv=dv),
        grid=(B, per_b),
        in_specs=[
            pl.BlockSpec((lc, D), lambda b, s: (row(b, s), 0)),
            pl.BlockSpec((lc, D), lambda b, s: (row(b, s), 1)),
            pl.BlockSpec((lc, 2 * D), lambda b, s: (row(b, s), 1)),
            pl.BlockSpec((lc, 2 * D), lambda b, s: (row(b, s), 0)),
            pl.BlockSpec((H, CHUNK, CHUNK), lambda b, s: (0, 0, 0)),
            pl.BlockSpec((H, CHUNK, 1), lambda b, s: (0, 0, 0)),
            pl.BlockSpec((H, CHUNK, 1), lambda b, s: (0, 0, 0)),
            pl.BlockSpec((H, 1, 1), lambda b, s: (0, 0, 0)),
        ],
        out_specs=pl.BlockSpec((lc, 2 * D), lambda b, s: (row(b, s), 0)),
        out_shape=jax.ShapeDtypeStruct((T, 2 * D), BF16),
        scratch_shapes=[pltpu.VMEM((H, dk, dv), F32)],
        compiler_params=_params("parallel", "arbitrary"),
        name="retention",
    )(qkv, qkv, qkv, gate, intra, qd, kd, cd)


def _retention_tables():
    h = jnp.arange(RET_HEADS, dtype=F32)
    log_g = jnp.log(1.0 - 2.0 ** (-5.0 - h))
    idx = jnp.arange(CHUNK, dtype=F32)
    intra = jnp.exp(log_g[:, None, None] * jnp.abs(idx[:, None] - idx[None, :]))
    qd = jnp.exp(log_g[:, None] * (idx[None, :] + 1.0))[:, :, None]
    kd = jnp.exp(log_g[:, None] * (CHUNK - 1.0 - idx[None, :]))[:, :, None]
    cd = jnp.exp(log_g * CHUNK)[:, None, None]
    return intra, qd, kd, cd


def _rotary_tables(S, d):
    inv = 1.0 / (ROPE_BASE ** (jnp.arange(0, d, 2, dtype=F32) / d))
    ang = jnp.arange(S).astype(F32)[:, None] * inv[None, :]
    return jnp.cos(ang), jnp.sin(ang)


def _proj_res_kernel(*refs, gated):
    if gated:
        o_in, og_ref, w_ref, h_ref, gt_ref, out_ref = refs
        x = (o_in[...] * _sigmoid(og_ref[...])).astype(BF16)
    else:
        x_ref, w_ref, h_ref, gt_ref, out_ref = refs
        x = x_ref[...]
    out_ref[...] = h_ref[...] + gt_ref[...] * jnp.dot(x, w_ref[...], preferred_element_type=F32)


def _proj_residual(x, w, w_layer, h, gate, S, tm, og=None, og_block=0):
    T, D = h.shape
    K = w.shape[1]
    per_b = S // tm
    gated = og is not None
    xs = (x, og) if gated else (x,)
    in_specs = [pl.BlockSpec((tm, K), lambda i: (i, 0))]
    if gated:
        in_specs.append(pl.BlockSpec((tm, K), lambda i: (i, og_block)))
    in_specs += [
        pl.BlockSpec((None, K, D), lambda i: (w_layer, 0, 0)),
        pl.BlockSpec((tm, D), lambda i: (i, 0)),
        pl.BlockSpec((None, 1, D), lambda i: (i // per_b, 0, 0)),
    ]
    return pl.pallas_call(
        functools.partial(_proj_res_kernel, gated=gated),
        grid=(T // tm,),
        in_specs=in_specs,
        out_specs=pl.BlockSpec((tm, D), lambda i: (i, 0)),
        out_shape=jax.ShapeDtypeStruct((T, D), F32),
        compiler_params=_params("parallel"),
        name="proj_residual",
    )(*xs, w, h, gate)


def _ffn_dense_kernel(h_ref, g_ref, sh_ref, sc_ref, gt_ref, wg_ref, wu_ref, wd_ref, o_ref, a_scr,
                      acc_scr):
    f = pl.program_id(1)

    @pl.when(f == 0)
    def _():
        a_scr[...] = _norm_mod(h_ref[...], g_ref[...], sh_ref[...], sc_ref[...]).astype(BF16)
        acc_scr[...] = jnp.zeros_like(acc_scr)

    a = a_scr[...]
    gate = jnp.dot(a, wg_ref[...], preferred_element_type=F32)
    up = jnp.dot(a, wu_ref[...], preferred_element_type=F32)
    act = (_silu(gate) * up).astype(BF16)
    acc_scr[...] += jnp.dot(act, wd_ref[...], preferred_element_type=F32)

    @pl.when(f == pl.num_programs(1) - 1)
    def _():
        o_ref[...] = h_ref[...] + gt_ref[...] * acc_scr[...]


def _ffn_dense(h, g, sh, sc, gate, wg, wu, wd, layer, S, tm, tf):
    T, D = h.shape
    F = wg.shape[2]
    per_b = S // tm
    return pl.pallas_call(
        _ffn_dense_kernel,
        grid=(T // tm, F // tf),
        in_specs=[
            pl.BlockSpec((tm, D), lambda i, f: (i, 0)),
            pl.BlockSpec((1, D), lambda i, f: (0, 0)),
            pl.BlockSpec((None, 1, D), lambda i, f: (i // per_b, 0, 0)),
            pl.BlockSpec((None, 1, D), lambda i, f: (i // per_b, 0, 0)),
            pl.BlockSpec((None, 1, D), lambda i, f: (i // per_b, 0, 0)),
            pl.BlockSpec((None, D, tf), lambda i, f: (layer, 0, f)),
            pl.BlockSpec((None, D, tf), lambda i, f: (layer, 0, f)),
            pl.BlockSpec((None, tf, D), lambda i, f: (layer, f, 0)),
        ],
        out_specs=pl.BlockSpec((tm, D), lambda i, f: (i, 0)),
        out_shape=jax.ShapeDtypeStruct((T, D), F32),
        scratch_shapes=[pltpu.VMEM((tm, D), BF16), pltpu.VMEM((tm, D), F32)],
        compiler_params=_params("parallel", "arbitrary"),
        name="ffn_dense",
    )(h, g, sh, sc, gate, wg, wu, wd)


def _ffn_moe_kernel(ut_ref, ue_ref, ulo_ref, uhi_ref, ufirst_ref, uvalid_ref, x_ref, wg_ref, wu_ref,
                    wd_ref, o_ref, acc_scr):
    u = pl.program_id(0)
    f = pl.program_id(1)
    valid = uvalid_ref[u] > 0

    @pl.when(jnp.logical_and(jnp.logical_and(valid, ufirst_ref[u] > 0), f == 0))
    def _():
        acc_scr[...] = jnp.zeros_like(acc_scr)

    @pl.when(valid)
    def _():
        x = x_ref[...]
        gate = jnp.dot(x, wg_ref[...], preferred_element_type=F32)
        up = jnp.dot(x, wu_ref[...], preferred_element_type=F32)
        row = lax.broadcasted_iota(jnp.int32, gate.shape, 0)
        mine = jnp.logical_and(row >= ulo_ref[u], row < uhi_ref[u])
        act = jnp.where(mine, _silu(gate) * up, 0.0).astype(BF16)
        acc_scr[...] += jnp.dot(act, wd_ref[...], preferred_element_type=F32)

    @pl.when(jnp.logical_and(valid, f == pl.num_programs(1) - 1))
    def _():
        o_ref[...] = acc_scr[...]


def _ffn_moe(units, xs, wg, wu, wd, layer, tm, tf):
    N, D = xs.shape
    F = wg.shape[3]
    nf = F // tf
    n_units = units[0].shape[0]

    def xrow(u, f, ut, ue, ulo, uhi, ufirst, uvalid):
        return (ut[u], 0)

    def wcol(u, f, ut, ue, ulo, uhi, ufirst, uvalid):
        return (layer, ue[u], 0, jnp.where(uvalid[u] > 0, f, nf - 1))

    def wrow(u, f, ut, ue, ulo, uhi, ufirst, uvalid):
        return (layer, ue[u], jnp.where(uvalid[u] > 0, f, nf - 1), 0)

    grid_spec = pltpu.PrefetchScalarGridSpec(
        num_scalar_prefetch=6,
        grid=(n_units, nf),
        in_specs=[
            pl.BlockSpec((tm, D), xrow),
            pl.BlockSpec((None, None, D, tf), wcol),
            pl.BlockSpec((None, None, D, tf), wcol),
            pl.BlockSpec((None, None, tf, D), wrow),
        ],
        out_specs=pl.BlockSpec((tm, D), xrow),
        scratch_shapes=[pltpu.VMEM((tm, D), F32)],
    )
    return pl.pallas_call(
        _ffn_moe_kernel,
        grid_spec=grid_spec,
        out_shape=jax.ShapeDtypeStruct((N, D), F32),
        compiler_params=_params("arbitrary", "arbitrary"),
        name="ffn_moe",
    )(*units, xs, wg, wu, wd)


def _combine_kernel(h_ref, ya_ref, yb_ref, wt_ref, gt_ref, o_ref):
    wt = wt_ref[...]
    ff = wt[:, 0:1] * ya_ref[...] + wt[:, 1:2] * yb_ref[...]
    o_ref[...] = h_ref[...] + gt_ref[...] * ff


def _moe_combine(h, ya, yb, wt8, gate, S, tm):
    T, D = h.shape
    per_b = S // tm
    row = pl.BlockSpec((tm, D), lambda i: (i, 0))
    return pl.pallas_call(
        _combine_kernel,
        grid=(T // tm,),
        in_specs=[
            row, row, row,
            pl.BlockSpec((tm, 8), lambda i: (i, 0)),
            pl.BlockSpec((None, 1, D), lambda i: (i // per_b, 0, 0)),
        ],
        out_specs=row,
        out_shape=jax.ShapeDtypeStruct((T, D), F32),
        compiler_params=_params("parallel"),
        name="moe_combine",
    )(h, ya, yb, wt8, gate)


def _router_kernel(h_ref, g_ref, sh_ref, sc_ref, rw_ref, rb_ref, m_ref, idx_ref, wt_ref, *, n_e):
    a = _norm_mod(h_ref[...], g_ref[...], sh_ref[...], sc_ref[...]).astype(BF16)
    m_ref[...] = a
    logits = jnp.dot(a, rw_ref[...], preferred_element_type=F32) + rb_ref[...]
    lane = lax.broadcasted_iota(jnp.int32, logits.shape, 1)
    lane_f = lane.astype(F32)
    logits = jnp.where(lane < n_e, logits, NEG_BIG)
    m1 = jnp.max(logits, axis=-1, keepdims=True)
    i1 = jnp.min(jnp.where(logits == m1, lane_f, float(LANES)), axis=-1, keepdims=True)
    rest = jnp.where(lane_f == i1, NEG_BIG, logits)
    m2 = jnp.max(rest, axis=-1, keepdims=True)
    i2 = jnp.min(jnp.where(rest == m2, lane_f, float(LANES)), axis=-1, keepdims=True)
    e2 = jnp.exp(m2 - m1)
    w1 = 1.0 / (1.0 + e2)
    w2 = e2 * w1
    w = jnp.where(lane == 0, w1, jnp.where(lane == 1, w2, 0.0))
    ii = jnp.where(lane == 0, i1, jnp.where(lane == 1, i2, 0.0)).astype(jnp.int32)
    wt_ref[...] = w[:, :8]
    idx_ref[...] = ii[:, :8]


def _router(h, g, sh, sc, rw, rb, S, tm, n_e):
    T, D = h.shape
    per_b = S // tm
    return pl.pallas_call(
        functools.partial(_router_kernel, n_e=n_e),
        grid=(T // tm,),
        in_specs=[
            pl.BlockSpec((tm, D), lambda i: (i, 0)),
            pl.BlockSpec((1, D), lambda i: (0, 0)),
            pl.BlockSpec((None, 1, D), lambda i: (i // per_b, 0, 0)),
            pl.BlockSpec((None, 1, D), lambda i: (i // per_b, 0, 0)),
            pl.BlockSpec((D, LANES), lambda i: (0, 0)),
            pl.BlockSpec((1, LANES), lambda i: (0, 0)),
        ],
        out_specs=[
            pl.BlockSpec((tm, D), lambda i: (i, 0)),
            pl.BlockSpec((tm, 8), lambda i: (i, 0)),
            pl.BlockSpec((tm, 8), lambda i: (i, 0)),
        ],
        out_shape=[
            jax.ShapeDtypeStruct((T, D), BF16),
            jax.ShapeDtypeStruct((T, 8), jnp.int32),
            jax.ShapeDtypeStruct((T, 8), F32),
        ],
        compiler_params=_params("parallel"),
        name="router",
    )(h, g, sh, sc, rw, rb)


def _forget_kernel(h_ref, g_ref, sh_ref, sc_ref, wf_ref, bf_ref, o_ref, carry_ref, *, per_b):
    @pl.when(pl.program_id(0) % per_b == 0)
    def _():
        carry_ref[...] = jnp.zeros_like(carry_ref)

    a = _norm_mod(h_ref[...], g_ref[...], sh_ref[...], sc_ref[...]).astype(BF16)
    z = jnp.dot(a, wf_ref[...], preferred_element_type=F32) + bf_ref[...]
    lf = jnp.minimum(z, 0.0) - jnp.log(1.0 + jnp.exp(-jnp.abs(z)))
    tm = lf.shape[0]
    r = lax.broadcasted_iota(jnp.int32, (tm, tm), 0)
    c = lax.broadcasted_iota(jnp.int32, (tm, tm), 1)
    tri = jnp.where(c <= r, 1.0, 0.0).astype(F32)
    cs = jnp.dot(tri, lf, preferred_element_type=F32, precision=lax.Precision.HIGHEST)
    out = cs + carry_ref[...]
    carry_ref[...] = out[tm - 1:tm, :]
    out2 = out * LOG2E
    hi = out2.astype(BF16)
    r1 = out2 - hi.astype(F32)
    mid = r1.astype(BF16)
    lo = (r1 - mid.astype(F32)).astype(BF16)
    o_ref[:, 0:LANES] = hi
    o_ref[:, LANES:2 * LANES] = mid
    o_ref[:, 2 * LANES:3 * LANES] = lo


def _forget_cumsum(h, g, sh, sc, wf, bf, S, tm):
    T, D = h.shape
    per_b = S // tm
    return pl.pallas_call(
        functools.partial(_forget_kernel, per_b=per_b),
        grid=(T // tm,),
        in_specs=[
            pl.BlockSpec((tm, D), lambda i: (i, 0)),
            pl.BlockSpec((1, D), lambda i: (0, 0)),
            pl.BlockSpec((None, 1, D), lambda i: (i // per_b, 0, 0)),
            pl.BlockSpec((None, 1, D), lambda i: (i // per_b, 0, 0)),
            pl.BlockSpec((D, LANES), lambda i: (0, 0)),
            pl.BlockSpec((1, LANES), lambda i: (0, 0)),
        ],
        out_specs=pl.BlockSpec((tm, 3 * LANES), lambda i: (i, 0)),
        out_shape=jax.ShapeDtypeStruct((T, 3 * LANES), BF16),
        scratch_shapes=[pltpu.VMEM((1, LANES), F32)],
        compiler_params=_params("arbitrary"),
        name="forget_cumsum",
    )(h, g, sh, sc, wf, bf)


def _fox_prep_kernel(kp_ref, vp_ref, f_ref, pm_ref, ka_ref, vt_ref, kn_ref, *, dh, tk):
    c = pl.program_id(2)
    n_sub = kp_ref.shape[0] // tk
    lane = lax.broadcasted_iota(jnp.int32, (tk, LANES), 1)
    lane8 = lax.broadcasted_iota(jnp.int32, (8, LANES), 1)
    row8 = lax.broadcasted_iota(jnp.int32, (8, LANES), 0)
    row16 = lax.broadcasted_iota(jnp.int32, (DV_ROWS - dh, tk), 0)
    tail = jnp.where(row16 == 0, 1.0, 0.0).astype(BF16)

    @pl.when(c == 0)
    def _():
        kn_ref[...] = jnp.zeros_like(kn_ref)

    for j in range(n_sub):
        rows = slice(j * tk, (j + 1) * tk)
        kp = kp_ref[rows, :]
        ex = jnp.dot(f_ref[rows, :], pm_ref[...], preferred_element_type=F32)
        ex = jnp.where((lane % dh) < 3, 1.0, ex).astype(BF16)
        ka_ref[0, rows, :] = jnp.where(lane < dh, kp, ex)
        ka_ref[1, rows, :] = jnp.where(lane < dh, ex, kp)
        vt = vp_ref[rows, :].astype(F32).T
        vt_ref[0, 0:dh, rows] = vt[0:dh].astype(BF16)
        vt_ref[1, 0:dh, rows] = vt[dh:2 * dh].astype(BF16)
        vt_ref[0, dh:DV_ROWS, rows] = tail
        vt_ref[1, dh:DV_ROWS, rows] = tail
        k2 = kp.astype(F32)
        k2 = k2 * k2
        na = jnp.max(jnp.sum(jnp.where(lane < dh, k2, 0.0), axis=1, keepdims=True), axis=0, keepdims=True)
        nb = jnp.max(jnp.sum(jnp.where(lane < dh, 0.0, k2), axis=1, keepdims=True), axis=0, keepdims=True)
        upd = jnp.where(row8 == 0, na, jnp.where(row8 == 1, nb, 0.0))
        kn_ref[...] = jnp.where(lane8 == c * n_sub + j, upd, kn_ref[...])


def _fox_prep(kv, fparts, pm, B, S, D, dh, tk, tp):
    H = D // dh
    nb = S // tp
    assert S // tk <= LANES and tp % tk == 0
    return pl.pallas_call(
        functools.partial(_fox_prep_kernel, dh=dh, tk=tk),
        grid=(B, H // 2, nb),
        in_specs=[
            pl.BlockSpec((tp, LANES), lambda b, p, c: (b * nb + c, p)),
            pl.BlockSpec((tp, LANES), lambda b, p, c: (b * nb + c, D // LANES + p)),
            pl.BlockSpec((tp, 3 * LANES), lambda b, p, c: (b * nb + c, 0)),
            pl.BlockSpec((None, 3 * LANES, LANES), lambda b, p, c: (p, 0, 0)),
        ],
        out_specs=[
            pl.BlockSpec((None, 2, tp, LANES), lambda b, p, c: (b, p, c, 0)),
            pl.BlockSpec((None, 2, DV_ROWS, tp), lambda b, p, c: (b, p, 0, c)),
            pl.BlockSpec((None, None, 8, LANES), lambda b, p, c: (b, p, 0, 0)),
        ],
        out_shape=[
            jax.ShapeDtypeStruct((B, H, S, LANES), BF16),
            jax.ShapeDtypeStruct((B, H, DV_ROWS, S), BF16),
            jax.ShapeDtypeStruct((B, H // 2, 8, LANES), F32),
        ],
        compiler_params=_params("parallel", "parallel", "arbitrary"),
        name="fox_prep",
    )(kv, kv, fparts, pm)


def _fox_kernel(q_ref, gq_ref, gb_ref, kn_ref, ka_ref, vt_ref, o_ref, sa0, sa1, sb0, sb1, *,
                tq, tk, dh):
    i = pl.program_id(2)
    qT = q_ref[...].T
    g16 = gq_ref[...]
    row8 = lax.broadcasted_iota(jnp.int32, (8, tq), 0)
    xa = jnp.where(row8 < 3, g16[0:8], jnp.where(row8 < 6, 1.0, 0.0))
    xb = jnp.where(row8 < 3, g16[8:16], jnp.where(row8 < 6, 1.0, 0.0))
    zpad = jnp.zeros((LANES - dh - 8, tq), F32)
    qa = jnp.concatenate([qT[0:dh], xa, zpad], axis=0).astype(BF16)
    qb = jnp.concatenate([xb, zpad, qT[dh:2 * dh]], axis=0).astype(BF16)

    def scores(h, q, blk):
        k0 = pl.multiple_of(blk * tk, tk)
        return jnp.dot(ka_ref[h, pl.ds(k0, tk), :], q, preferred_element_type=F32)

    def update(s_ref, h, blk, m, acc):
        k0 = pl.multiple_of(blk * tk, tk)
        s = s_ref[...]
        m_new = jnp.maximum(m, jnp.max(s, axis=0, keepdims=True))
        alpha = jnp.exp2(m - m_new)
        p = jnp.exp2(s - m_new).astype(BF16)
        pv = jnp.dot(vt_ref[h, :, pl.ds(k0, tk)], p, preferred_element_type=F32)
        return m_new, alpha * acc + pv

    key = lax.broadcasted_iota(jnp.int32, (tk, tq), 0)
    qry = lax.broadcasted_iota(jnp.int32, (tk, tq), 1)
    sa0[...] = jnp.where(key <= qry, scores(0, qa, i), NEG_BIG)
    sb0[...] = jnp.where(key <= qry, scores(1, qb, i), NEG_BIG)
    ma0 = jnp.max(sa0[...], axis=0, keepdims=True)
    mb0 = jnp.max(sb0[...], axis=0, keepdims=True)

    q2 = qT * qT
    qna = jnp.max(jnp.sum(q2[0:dh], axis=0, keepdims=True), axis=1, keepdims=True)
    qnb = jnp.max(jnp.sum(q2[dh:2 * dh], axis=0, keepdims=True), axis=1, keepdims=True)
    kn = kn_ref[...]
    gb = gb_ref[...]
    lane = lax.broadcasted_iota(jnp.int32, (1, LANES), 1)
    at_i = lane == i

    def pick(row):
        return jnp.sum(jnp.where(at_i, row, 0.0), axis=1, keepdims=True)

    def needed(qn, knr, glast, gfirst, m0):
        floor = jnp.min(m0, axis=1, keepdims=True)
        bound = NORM_SLACK * jnp.sqrt(qn * knr) + pick(gfirst) - glast - floor
        return bound > -SKIP_LOG2

    need = jnp.logical_or(needed(qna, kn[0:1], gb[0:1], gb[2:3], ma0),
                          needed(qnb, kn[1:2], gb[1:2], gb[3:4], mb0))
    lane_f = lane.astype(F32)
    i_f = i.astype(F32)
    cand = jnp.where(jnp.logical_and(need, lane < i), lane_f, i_f)
    jmin = jnp.min(cand).astype(jnp.int32)

    n_off = i - jmin
    n_pairs = (n_off + 2) // 2

    def pair(u, carry):
        ma, acca, mb, accb = carry
        b1 = jmin + 2 * u
        b0 = jnp.where(u == 0, i, b1 - 1)
        sa1[...] = scores(0, qa, b1)
        sb1[...] = scores(1, qb, b1)
        ma, acca = update(sa0, 0, b0, ma, acca)
        mb, accb = update(sb0, 1, b0, mb, accb)
        sa0[...] = scores(0, qa, b1 + 1)
        sb0[...] = scores(1, qb, b1 + 1)
        ma, acca = update(sa1, 0, b1, ma, acca)
        mb, accb = update(sb1, 1, b1, mb, accb)
        return ma, acca, mb, accb

    m0 = jnp.full((1, tq), NEG_BIG, F32)
    a0 = jnp.zeros((DV_ROWS, tq), F32)
    ma, acca, mb, accb = lax.fori_loop(0, n_pairs - 1, pair, (m0, a0, m0, a0))
    u = n_pairs - 1
    real = 2 * u < n_off
    b1 = jnp.where(real, jmin + 2 * u, i)
    b0 = jnp.where(u == 0, i, jmin + 2 * u - 1)
    sa1[...] = jnp.where(real, scores(0, qa, b1), NEG_BIG)
    sb1[...] = jnp.where(real, scores(1, qb, b1), NEG_BIG)
    ma, acca = update(sa0, 0, b0, ma, acca)
    mb, accb = update(sb0, 1, b0, mb, accb)
    ma, acca = update(sa1, 0, b1, ma, acca)
    mb, accb = update(sb1, 1, b1, mb, accb)
    oa = acca[:dh, :] * (1.0 / acca[dh:dh + 1, :])
    ob = accb[:dh, :] * (1.0 / accb[dh:dh + 1, :])
    o_ref[...] = jnp.concatenate([oa, ob], axis=0).T


def _fox_attention(qg, gq, gb, kn, ka, vt, B, S, D, dh, tq):
    T = B * S
    H = D // dh
    nq = S // tq
    return pl.pallas_call(
        functools.partial(_fox_kernel, tq=tq, tk=tq, dh=dh),
        grid=(B, H // 2, nq),
        in_specs=[
            pl.BlockSpec((tq, LANES), lambda b, p, i: (b * nq + i, p)),
            pl.BlockSpec((None, None, 16, tq), lambda b, p, i: (b, p, 0, i)),
            pl.BlockSpec((None, None, 8, LANES), lambda b, p, i: (b, p, 0, 0)),
            pl.BlockSpec((None, None, 8, LANES), lambda b, p, i: (b, p, 0, 0)),
            pl.BlockSpec((None, 2, S, LANES), lambda b, p, i: (b, p, 0, 0)),
            pl.BlockSpec((None, 2, DV_ROWS, S), lambda b, p, i: (b, p, 0, 0)),
        ],
        out_specs=pl.BlockSpec((tq, LANES), lambda b, p, i: (b * nq + i, p)),
        out_shape=jax.ShapeDtypeStruct((T, D), F32),
        scratch_shapes=[pltpu.VMEM((tq, tq), F32) for _ in range(4)],
        compiler_params=_params("parallel", "parallel", "arbitrary"),
        name="fox_attention",
    )(qg, gq, gb, kn, ka, vt)


def _fox_gate_tables(fparts, B, S, H, tk):
    f3 = fparts.reshape(B, S, 3, LANES)[:, :, :, :H].astype(F32)
    gq = f3.reshape(B, S, 3, H // 2, 2).transpose(0, 3, 4, 2, 1)
    gq = jnp.pad(gq, ((0, 0), (0, 0), (0, 0), (0, 5), (0, 0))).reshape(B, H // 2, 16, S)
    G = jnp.sum(f3, axis=2)
    nb = S // tk
    Gb = G.reshape(B, nb, tk, H // 2, 2)
    last = Gb[:, :, tk - 1].transpose(0, 2, 3, 1)
    first = Gb[:, :, 0].transpose(0, 2, 3, 1)
    gb = jnp.concatenate([last, first, jnp.zeros_like(last), jnp.zeros_like(last)], axis=2)
    gb = jnp.pad(gb, ((0, 0), (0, 0), (0, 0), (0, LANES - nb)))
    return gq, gb


def _fox_placement(H, dh):
    p = jnp.arange(H // 2)[:, None, None]
    r = jnp.arange(3 * LANES)[None, :, None]
    l = jnp.arange(LANES)[None, None, :]
    j = r // LANES
    head = r % LANES
    hit_b = jnp.logical_and(head == 2 * p + 1, l == 3 + j)
    hit_a = jnp.logical_and(head == 2 * p, l == dh + 3 + j)
    return jnp.where(jnp.logical_or(hit_a, hit_b), -1.0, 0.0).astype(BF16)


def _final_kernel(h_ref, g_ref, sh_ref, sc_ref, o_ref):
    o_ref[...] = _norm_mod(h_ref[...], g_ref[...], sh_ref[...], sc_ref[...])


def _final_norm(h, g, sh, sc, S, tm):
    T, D = h.shape
    per_b = S // tm
    return pl.pallas_call(
        _final_kernel,
        grid=(T // tm,),
        in_specs=[
            pl.BlockSpec((tm, D), lambda i: (i, 0)),
            pl.BlockSpec((1, D), lambda i: (0, 0)),
            pl.BlockSpec((None, 1, D), lambda i: (i // per_b, 0, 0)),
            pl.BlockSpec((None, 1, D), lambda i: (i // per_b, 0, 0)),
        ],
        out_specs=pl.BlockSpec((tm, D), lambda i: (i, 0)),
        out_shape=jax.ShapeDtypeStruct((T, D), F32),
        compiler_params=_params("parallel"),
        name="final_norm",
    )(h, g, sh, sc)


def _dispatch_tables(top_idx, n_e, tm):
    T, K = top_idx.shape
    n = T * K
    n_tiles = n // tm
    n_units = n_tiles + n_e
    e_flat = top_idx.reshape(n)
    order = jnp.argsort(e_flat, stable=True).astype(jnp.int32)
    pos = jnp.argsort(order).astype(jnp.int32)
    ids = jnp.arange(n_e, dtype=jnp.int32)
    counts = jnp.sum(e_flat[:, None] == ids[None, :], axis=0).astype(jnp.int32)
    ends = jnp.cumsum(counts)
    starts = ends - counts
    t0 = jnp.arange(n_tiles, dtype=jnp.int32)[:, None] * tm
    present = jnp.logical_and(starts[None, :] < t0 + tm, ends[None, :] > t0)
    present = jnp.logical_and(present, counts[None, :] > 0)
    flat = jnp.arange(n_tiles * n_e, dtype=jnp.int32)
    keys = jnp.sort(jnp.where(present.reshape(-1), flat, n_tiles * n_e))[:n_units]
    valid = keys < n_tiles * n_e
    n_valid = jnp.sum(valid.astype(jnp.int32))
    keys = jnp.where(valid, keys, keys[jnp.maximum(n_valid - 1, 0)])
    ut = keys // n_e
    ue = keys % n_e
    ulo = jnp.clip(starts[ue] - ut * tm, 0, tm)
    uhi = jnp.clip(ends[ue] - ut * tm, 0, tm)
    prev_t = jnp.concatenate([jnp.full((1,), -1, jnp.int32), ut[:-1]])
    ufirst = (ut != prev_t).astype(jnp.int32)
    units = tuple(a.astype(jnp.int32) for a in (ut, ue, ulo, uhi, ufirst, valid))
    return order, pos.reshape(T, K), units


def _tile(S, want):
    t = min(S, want)
    assert S % t == 0
    return t


def kernel(x, c, ada_w, ada_b, norm_g, ret_w_in, ret_w_o, kv_ada_w, kv_ada_b, kv_norm_g, fox_w_kv, fox_w_f, fox_b_f, fox_w_qg, fox_w_o, ffn_w_gate, ffn_w_up, ffn_w_down, router_w, router_b, moe_w_gate, moe_w_up, moe_w_down, final_ada_w, final_ada_b, final_norm_g):
    B, S, D = x.shape
    T = B * S
    depth = ada_w.shape[0]
    n_a = ret_w_in.shape[0]
    n_e = router_w.shape[-1]
    dh = D // FOX_HEADS
    dk = D // RET_HEADS
    assert dk == 2 * LANES and S % CHUNK == 0 and n_e <= 8

    tm = _tile(S, 512)
    tm_ffn = _tile(S, 1024)
    tm_moe = _tile(S, 1024)
    tq = _tile(S, 512)
    lc = _tile(S, 512)
    f_dense = ffn_w_gate.shape[-1]
    f_moe = moe_w_gate.shape[-1]
    tf_dense = f_dense // 2 if (f_dense // 2) % LANES == 0 else f_dense
    tf_moe = 512 if f_moe % 512 == 0 else f_moe

    c_pad = jnp.zeros((8, D), F32).at[:B].set(c)
    ada = _ada(c_pad, ada_w, ada_b[:, None, :])[:, :B]
    extra_w = jnp.stack([kv_ada_w, final_ada_w])
    extra_b = jnp.stack([kv_ada_b, final_ada_b])[:, None, :]
    extra = _ada(c_pad, extra_w, extra_b)[:, :B]
    vec = lambda a: a[:, None, :]

    h = x.reshape(T, D)
    cos, sin = _rotary_tables(S, dk)
    ret_tabs = _retention_tables()
    shared = None
    w_in_b, w_ro_b = ret_w_in.astype(BF16), ret_w_o.astype(BF16)
    w_kv_b, w_qg_b, w_fo_b = fox_w_kv.astype(BF16)[None], fox_w_qg.astype(BF16), fox_w_o.astype(BF16)
    w_fg_b, w_fu_b, w_fd_b = ffn_w_gate.astype(BF16), ffn_w_up.astype(BF16), ffn_w_down.astype(BF16)
    w_mg_b, w_mu_b, w_md_b = moe_w_gate.astype(BF16), moe_w_up.astype(BF16), moe_w_down.astype(BF16)

    for l in range(depth):
        sh1, sc1, g1, sh2, sc2, g2 = [vec(a) for a in jnp.split(ada[l], 6, axis=-1)]
        ng1 = norm_g[l, 0][None, :]
        ng2 = norm_g[l, 1][None, :]
        if l == n_a:
            kv_sh, kv_sc = [vec(a) for a in jnp.split(extra[0], 2, axis=-1)]
            kvg = kv_norm_g[None, :]
            kv = _norm_mod_matmul(h, kvg, kv_sh, kv_sc, w_kv_b, S,
                                  modes=("plain",) * 4, scales=(1.0,) * 4, cw=D // 2,
                                  out_dtype=BF16, tm=tm)
            wf = jnp.zeros((D, LANES), F32).at[:, :FOX_HEADS].set(fox_w_f).astype(BF16)
            bf = jnp.zeros((1, LANES), F32).at[0, :FOX_HEADS].set(fox_b_f)
            fparts = _forget_cumsum(h, kvg, kv_sh, kv_sc, wf, bf, S, tm)
            ka, vt, kn = _fox_prep(kv, fparts, _fox_placement(FOX_HEADS, dh), B, S, D, dh, tq,
                                   _tile(S, 4 * tq))
            gq, gb = _fox_gate_tables(fparts, B, S, FOX_HEADS, tq)
            shared = (gq, gb, kn, ka, vt)
        if l < n_a:
            qkv = _norm_mod_matmul(
                h, ng1, sh1, sc1, w_in_b, S,
                modes=("rot",) * 4 + ("plain",) * 4,
                scales=(1.0,) * 2 + (dk ** -0.5,) * 2 + (1.0,) * 4,
                cw=D // 2, out_dtype=BF16, tm=tm, rot=(cos, sin), w_layer=l, w_col=0)
            gate = _norm_mod_matmul(h, ng1, sh1, sc1, w_in_b, S,
                                    modes=("plain",) * 4, scales=(1.0,) * 4, cw=D // 2,
                                    out_dtype=F32, tm=tm, w_layer=l, w_col=2)
            y = _retention(qkv, gate, ret_tabs, B, S, lc)
            h = _proj_residual(y, w_ro_b, l, h, g1, S, tm)
        else:
            j = l - n_a
            qg = _norm_mod_matmul(h, ng1, sh1, sc1, w_qg_b, S,
                                  modes=("plain",) * 4, scales=(dh ** -0.5 * LOG2E,) * 2 + (1.0,) * 2,
                                  cw=D // 2, out_dtype=F32, tm=tm, w_layer=j)
            o = _fox_attention(qg, *shared, B, S, D, dh, tq)
            h = _proj_residual(o, w_fo_b, j, h, g1, S, tm, og=qg, og_block=1)
        i = l // 2
        if l % 2 == 0:
            h = _ffn_dense(h, ng2, sh2, sc2, g2, w_fg_b, w_fu_b, w_fd_b, i, S, tm_ffn, tf_dense)
        else:
            rw = jnp.zeros((D, LANES), F32).at[:, :n_e].set(router_w[i]).astype(BF16)
            rb = jnp.zeros((1, LANES), F32).at[0, :n_e].set(router_b[i])
            m, idx8, wt8 = _router(h, ng2, sh2, sc2, rw, rb, S, tm, n_e)
            order, pos, units = _dispatch_tables(idx8[:, :TOP_K], n_e, tm_moe)
            xs = m.at[order // TOP_K].get(mode="promise_in_bounds")
            ys = _ffn_moe(units, xs, w_mg_b, w_mu_b, w_md_b, i, tm_moe, tf_moe)
            ya = ys.at[pos[:, 0]].get(mode="promise_in_bounds")
            yb = ys.at[pos[:, 1]].get(mode="promise_in_bounds")
            h = _moe_combine(h, ya, yb, wt8, g2, S, tm)

    f_sh, f_sc = [vec(a) for a in jnp.split(extra[1], 2, axis=-1)]
    out = _final_norm(h, final_norm_g[None, :], f_sh, f_sc, S, tm)
    return out.reshape(B, S, D)
```

```python
import functools
import math

import jax
import jax.numpy as jnp
from jax import lax
from jax.experimental import pallas as pl
from jax.experimental.pallas import tpu as pltpu

F32 = jnp.float32
BF16 = jnp.bfloat16

CHUNK = 64
RET_HEADS = 4
FOX_HEADS = 16
ROPE_BASE = 10000.0
TOP_K = 2
EPS = 1e-6

LANES = 128
VMEM_LIMIT_BYTES = 56 * 1024 * 1024
NEG_BIG = -1e30
DV_ROWS = 80
SKIP_LOG2 = 160.0
NORM_SLACK = 1.02
LOG2E = math.log2(math.e)


def _params(*sem):
    return pltpu.CompilerParams(dimension_semantics=sem, vmem_limit_bytes=VMEM_LIMIT_BYTES)


def _silu(x):
    return x * (1.0 / (1.0 + jnp.exp(-x)))


def _sigmoid(x):
    return 1.0 / (1.0 + jnp.exp(-x))


def _norm_mod(x, g, sh, sc):
    var = jnp.mean(x * x, axis=-1, keepdims=True)
    y = (x * lax.rsqrt(var + EPS)) * g
    return y * (1.0 + sc) + sh


def _ada_kernel(c_ref, w_ref, b_ref, o_ref):
    ca = _silu(c_ref[...]).astype(BF16)
    o_ref[...] = jnp.dot(ca, w_ref[...].astype(BF16), preferred_element_type=F32) + b_ref[...]


def _ada(c_pad, w, b, tn=1024):
    L, D, N = w.shape
    return pl.pallas_call(
        _ada_kernel,
        grid=(L, N // tn),
        in_specs=[
            pl.BlockSpec((8, D), lambda l, j: (0, 0)),
            pl.BlockSpec((None, D, tn), lambda l, j: (l, 0, j)),
            pl.BlockSpec((None, 1, tn), lambda l, j: (l, 0, j)),
        ],
        out_specs=pl.BlockSpec((None, 8, tn), lambda l, j: (l, 0, j)),
        out_shape=jax.ShapeDtypeStruct((L, 8, N), F32),
        compiler_params=_params("parallel", "parallel"),
        name="ada_proj",
    )(c_pad, w, b)


def _nmm_kernel(*refs, n_chunks, cw, modes, scales, out_dtype):
    h_ref, g_ref, sh_ref, sc_ref, w_ref = refs[:5]
    has_rot = any(m == "rot" for m in modes)
    if has_rot:
        cos_ref, sin_ref, o_ref = refs[5:8]
    else:
        o_ref = refs[5]
    a = _norm_mod(h_ref[...], g_ref[...], sh_ref[...], sc_ref[...]).astype(BF16)
    for c in range(n_chunks):
        acc = jnp.dot(a, w_ref[:, c * cw:(c + 1) * cw], preferred_element_type=F32)
        if modes[c] == "rot":
            cos = cos_ref[...]
            sin = sin_ref[...]
            parts = []
            for hh in range(cw // (2 * LANES)):
                x1 = acc[:, (2 * hh) * LANES:(2 * hh + 1) * LANES]
                x2 = acc[:, (2 * hh + 1) * LANES:(2 * hh + 2) * LANES]
                parts.append(x1 * cos - x2 * sin)
                parts.append(x2 * cos + x1 * sin)
            acc = jnp.concatenate(parts, axis=-1)
        if scales[c] != 1.0:
            acc = acc * scales[c]
        o_ref[:, c * cw:(c + 1) * cw] = acc.astype(out_dtype)


def _norm_mod_matmul(h, g, sh, sc, w, S, *, modes, scales, cw, out_dtype, tm, rot=None, w_layer=0,
                     w_col=0):
    T, D = h.shape
    n_chunks = len(modes)
    N = n_chunks * cw
    per_b = S // tm
    in_specs = [
        pl.BlockSpec((tm, D), lambda i: (i, 0)),
        pl.BlockSpec((1, D), lambda i: (0, 0)),
        pl.BlockSpec((None, 1, D), lambda i: (i // per_b, 0, 0)),
        pl.BlockSpec((None, 1, D), lambda i: (i // per_b, 0, 0)),
        pl.BlockSpec((None, D, N), lambda i: (w_layer, 0, w_col)),
    ]
    args = [h, g, sh, sc, w]
    if rot is not None:
        in_specs += [pl.BlockSpec((tm, LANES), lambda i: (i % per_b, 0))] * 2
        args += list(rot)
    return pl.pallas_call(
        functools.partial(_nmm_kernel, n_chunks=n_chunks, cw=cw, modes=modes, scales=scales,
                          out_dtype=out_dtype),
        grid=(T // tm,),
        in_specs=in_specs,
        out_specs=pl.BlockSpec((tm, N), lambda i: (i, 0)),
        out_shape=jax.ShapeDtypeStruct((T, N), out_dtype),
        compiler_params=_params("parallel"),
        name="norm_mod_matmul",
    )(*args)


def _ret_kernel(q_ref, k_ref, v_ref, g_ref, intra_ref, qd_ref, kd_ref, cd_ref, o_ref, st_ref, *,
                n_sub, dk, dv):
    @pl.when(pl.program_id(1) == 0)
    def _():
        st_ref[...] = jnp.zeros_like(st_ref)

    def body(c, carry):
        r0 = pl.multiple_of(c * CHUNK, CHUNK)
        rows = pl.ds(r0, CHUNK)
        for h in range(RET_HEADS):
            qc = q_ref[rows, h * dk:(h + 1) * dk]
            kc = k_ref[rows, h * dk:(h + 1) * dk]
            vc = v_ref[rows, h * dv:(h + 1) * dv]
            state = st_ref[h]
            scores = lax.dot_general(qc, kc, (((1,), (1,)), ((), ())),
                                     preferred_element_type=F32) * intra_ref[h]
            o = (jnp.dot(scores.astype(BF16), vc, preferred_element_type=F32)
                 + jnp.dot(qc, state.astype(BF16), preferred_element_type=F32) * qd_ref[h])
            kdk = (kc.astype(F32) * kd_ref[h]).astype(BF16)
            st_ref[h] = state * cd_ref[h] + lax.dot_general(kdk, vc, (((0,), (0,)), ((), ())),
                                                             preferred_element_type=F32)
            mu = jnp.mean(o, axis=-1, keepdims=True)
            d = o - mu
            var = jnp.mean(d * d, axis=-1, keepdims=True)
            y = d * lax.rsqrt(var + EPS)
            gate = g_ref[rows, h * dv:(h + 1) * dv]
            o_ref[rows, h * dv:(h + 1) * dv] = (_silu(gate) * y).astype(o_ref.dtype)
        return carry

    lax.fori_loop(0, n_sub, body, 0)


def _retention(qkv, gate, tabs, B, S, lc):
    T = qkv.shape[0]
    D = qkv.shape[1] // 4
    dk = D // RET_HEADS
    dv = 2 * D // RET_HEADS
    H = RET_HEADS
    per_b = S // lc
    intra, qd, kd, cd = tabs
    row = lambda b, s: b * per_b + s
    return pl.pallas_call(
        functools.partial(_ret_kernel, n_sub=lc // CHUNK, dk=dk, dv=dv),
        grid=(B, per_b),
        in_specs=[
            pl.BlockSpec((lc, D), lambda b, s: (row(b, s), 0)),
            pl.BlockSpec((lc, D), lambda b, s: (row(b, s), 1)),
            pl.BlockSpec((lc, 2 * D), lambda b, s: (row(b, s), 1)),
            pl.BlockSpec((lc, 2 * D), lambda b, s: (row(b, s), 0)),
            pl.BlockSpec((H, CHUNK, CHUNK), lambda b, s: (0, 0, 0)),
            pl.BlockSpec((H, CHUNK, 1), lambda b, s: (0, 0, 0)),
            pl.BlockSpec((H, CHUNK, 1), lambda b, s: (0, 0, 0)),
            pl.BlockSpec((H, 1, 1), lambda b, s: (0, 0, 0)),
        ],
        out_specs=pl.BlockSpec((lc, 2 * D), lambda b, s: (row(b, s), 0)),
        out_shape=jax.ShapeDtypeStruct((T, 2 * D), BF16),
        scratch_shapes=[pltpu.VMEM((H, dk, dv), F32)],
        compiler_params=_params("parallel", "arbitrary"),
        name="retention",
    )(qkv, qkv, qkv, gate, intra, qd, kd, cd)


def _retention_tables():
    h = jnp.arange(RET_HEADS, dtype=F32)
    log_g = jnp.log(1.0 - 2.0 ** (-5.0 - h))
    idx = jnp.arange(CHUNK, dtype=F32)
    intra = jnp.exp(log_g[:, None, None] * jnp.abs(idx[:, None] - idx[None, :]))
    qd = jnp.exp(log_g[:, None] * (idx[None, :] + 1.0))[:, :, None]
    kd = jnp.exp(log_g[:, None] * (CHUNK - 1.0 - idx[None, :]))[:, :, None]
    cd = jnp.exp(log_g * CHUNK)[:, None, None]
    return intra, qd, kd, cd


def _rotary_tables(S, d):
    inv = 1.0 / (ROPE_BASE ** (jnp.arange(0, d, 2, dtype=F32) / d))
    ang = jnp.arange(S).astype(F32)[:, None] * inv[None, :]
    return jnp.cos(ang), jnp.sin(ang)


def _proj_res_kernel(*refs, gated):
    if gated:
        o_in, og_ref, w_ref, h_ref, gt_ref, out_ref = refs
        x = (o_in[...] * _sigmoid(og_ref[...])).astype(BF16)
    else:
        x_ref, w_ref, h_ref, gt_ref, out_ref = refs
        x = x_ref[...]
    out_ref[...] = h_ref[...] + gt_ref[...] * jnp.dot(x, w_ref[...], preferred_element_type=F32)


def _proj_residual(x, w, w_layer, h, gate, S, tm, og=None, og_block=0):
    T, D = h.shape
    K = w.shape[1]
    per_b = S // tm
    gated = og is not None
    xs = (x, og) if gated else (x,)
    in_specs = [pl.BlockSpec((tm, K), lambda i: (i, 0))]
    if gated:
        in_specs.append(pl.BlockSpec((tm, K), lambda i: (i, og_block)))
    in_specs += [
        pl.BlockSpec((None, K, D), lambda i: (w_layer, 0, 0)),
        pl.BlockSpec((tm, D), lambda i: (i, 0)),
        pl.BlockSpec((None, 1, D), lambda i: (i // per_b, 0, 0)),
    ]
    return pl.pallas_call(
        functools.partial(_proj_res_kernel, gated=gated),
        grid=(T // tm,),
        in_specs=in_specs,
        out_specs=pl.BlockSpec((tm, D), lambda i: (i, 0)),
        out_shape=jax.ShapeDtypeStruct((T, D), F32),
        compiler_params=_params("parallel"),
        name="proj_residual",
    )(*xs, w, h, gate)


def _ffn_dense_kernel(h_ref, g_ref, sh_ref, sc_ref, gt_ref, wg_ref, wu_ref, wd_ref, o_ref, a_scr,
                      acc_scr):
    f = pl.program_id(1)

    @pl.when(f == 0)
    def _():
        a_scr[...] = _norm_mod(h_ref[...], g_ref[...], sh_ref[...], sc_ref[...]).astype(BF16)
        acc_scr[...] = jnp.zeros_like(acc_scr)

    a = a_scr[...]
    gate = jnp.dot(a, wg_ref[...], preferred_element_type=F32)
    up = jnp.dot(a, wu_ref[...], preferred_element_type=F32)
    act = (_silu(gate) * up).astype(BF16)
    acc_scr[...] += jnp.dot(act, wd_ref[...], preferred_element_type=F32)

    @pl.when(f == pl.num_programs(1) - 1)
    def _():
        o_ref[...] = h_ref[...] + gt_ref[...] * acc_scr[...]


def _ffn_dense(h, g, sh, sc, gate, wg, wu, wd, layer, S, tm, tf):
    T, D = h.shape
    F = wg.shape[2]
    per_b = S // tm
    return pl.pallas_call(
        _ffn_dense_kernel,
        grid=(T // tm, F // tf),
        in_specs=[
            pl.BlockSpec((tm, D), lambda i, f: (i, 0)),
            pl.BlockSpec((1, D), lambda i, f: (0, 0)),
            pl.BlockSpec((None, 1, D), lambda i, f: (i // per_b, 0, 0)),
            pl.BlockSpec((None, 1, D), lambda i, f: (i // per_b, 0, 0)),
            pl.BlockSpec((None, 1, D), lambda i, f: (i // per_b, 0, 0)),
            pl.BlockSpec((None, D, tf), lambda i, f: (layer, 0, f)),
            pl.BlockSpec((None, D, tf), lambda i, f: (layer, 0, f)),
            pl.BlockSpec((None, tf, D), lambda i, f: (layer, f, 0)),
        ],
        out_specs=pl.BlockSpec((tm, D), lambda i, f: (i, 0)),
        out_shape=jax.ShapeDtypeStruct((T, D), F32),
        scratch_shapes=[pltpu.VMEM((tm, D), BF16), pltpu.VMEM((tm, D), F32)],
        compiler_params=_params("parallel", "arbitrary"),
        name="ffn_dense",
    )(h, g, sh, sc, gate, wg, wu, wd)


def _ffn_moe_kernel(ut_ref, ue_ref, ulo_ref, uhi_ref, ufirst_ref, uvalid_ref, x_ref, wg_ref, wu_ref,
                    wd_ref, o_ref, acc_scr):
    u = pl.program_id(0)
    f = pl.program_id(1)
    valid = uvalid_ref[u] > 0

    @pl.when(jnp.logical_and(jnp.logical_and(valid, ufirst_ref[u] > 0), f == 0))
    def _():
        acc_scr[...] = jnp.zeros_like(acc_scr)

    @pl.when(valid)
    def _():
        x = x_ref[...]
        gate = jnp.dot(x, wg_ref[...], preferred_element_type=F32)
        up = jnp.dot(x, wu_ref[...], preferred_element_type=F32)
        row = lax.broadcasted_iota(jnp.int32, gate.shape, 0)
        mine = jnp.logical_and(row >= ulo_ref[u], row < uhi_ref[u])
        act = jnp.where(mine, _silu(gate) * up, 0.0).astype(BF16)
        acc_scr[...] += jnp.dot(act, wd_ref[...], preferred_element_type=F32)

    @pl.when(jnp.logical_and(valid, f == pl.num_programs(1) - 1))
    def _():
        o_ref[...] = acc_scr[...]


def _ffn_moe(units, xs, wg, wu, wd, layer, tm, tf):
    N, D = xs.shape
    F = wg.shape[3]
    nf = F // tf
    n_units = units[0].shape[0]

    def xrow(u, f, ut, ue, ulo, uhi, ufirst, uvalid):
        return (ut[u], 0)

    def wcol(u, f, ut, ue, ulo, uhi, ufirst, uvalid):
        return (layer, ue[u], 0, jnp.where(uvalid[u] > 0, f, nf - 1))

    def wrow(u, f, ut, ue, ulo, uhi, ufirst, uvalid):
        return (layer, ue[u], jnp.where(uvalid[u] > 0, f, nf - 1), 0)

    grid_spec = pltpu.PrefetchScalarGridSpec(
        num_scalar_prefetch=6,
        grid=(n_units, nf),
        in_specs=[
            pl.BlockSpec((tm, D), xrow),
            pl.BlockSpec((None, None, D, tf), wcol),
            pl.BlockSpec((None, None, D, tf), wcol),
            pl.BlockSpec((None, None, tf, D), wrow),
        ],
        out_specs=pl.BlockSpec((tm, D), xrow),
        scratch_shapes=[pltpu.VMEM((tm, D), F32)],
    )
    return pl.pallas_call(
        _ffn_moe_kernel,
        grid_spec=grid_spec,
        out_shape=jax.ShapeDtypeStruct((N, D), F32),
        compiler_params=_params("arbitrary", "arbitrary"),
        name="ffn_moe",
    )(*units, xs, wg, wu, wd)


def _combine_kernel(*refs, fused_final):
    h_ref, ya_ref, yb_ref, wt_ref, gt_ref = refs[:5]
    o_ref = refs[-1]
    wt = wt_ref[...]
    ff = wt[:, 0:1] * ya_ref[...] + wt[:, 1:2] * yb_ref[...]
    out = h_ref[...] + gt_ref[...] * ff
    if fused_final:
        g_ref, sh_ref, sc_ref = refs[5:8]
        out = _norm_mod(out, g_ref[...], sh_ref[...], sc_ref[...])
    o_ref[...] = out


def _moe_combine(h, ya, yb, wt8, gate, S, tm, final=None):
    T, D = h.shape
    per_b = S // tm
    row = pl.BlockSpec((tm, D), lambda i: (i, 0))
    per_batch = pl.BlockSpec((None, 1, D), lambda i: (i // per_b, 0, 0))
    in_specs = [row, row, row, pl.BlockSpec((tm, 8), lambda i: (i, 0)), per_batch]
    args = [h, ya, yb, wt8, gate]
    if final is not None:
        in_specs += [pl.BlockSpec((1, D), lambda i: (0, 0)), per_batch, per_batch]
        args += list(final)
    return pl.pallas_call(
        functools.partial(_combine_kernel, fused_final=final is not None),
        grid=(T // tm,),
        in_specs=in_specs,
        out_specs=row,
        out_shape=jax.ShapeDtypeStruct((T, D), F32),
        compiler_params=_params("parallel"),
        name="moe_combine",
    )(*args)


def _router_kernel(h_ref, g_ref, sh_ref, sc_ref, rw_ref, rb_ref, m_ref, idx_ref, wt_ref, *, n_e):
    a = _norm_mod(h_ref[...], g_ref[...], sh_ref[...], sc_ref[...]).astype(BF16)
    m_ref[...] = a
    logits = jnp.dot(a, rw_ref[...], preferred_element_type=F32) + rb_ref[...]
    lane = lax.broadcasted_iota(jnp.int32, logits.shape, 1)
    lane_f = lane.astype(F32)
    logits = jnp.where(lane < n_e, logits, NEG_BIG)
    m1 = jnp.max(logits, axis=-1, keepdims=True)
    i1 = jnp.min(jnp.where(logits == m1, lane_f, float(LANES)), axis=-1, keepdims=True)
    rest = jnp.where(lane_f == i1, NEG_BIG, logits)
    m2 = jnp.max(rest, axis=-1, keepdims=True)
    i2 = jnp.min(jnp.where(rest == m2, lane_f, float(LANES)), axis=-1, keepdims=True)
    e2 = jnp.exp(m2 - m1)
    w1 = 1.0 / (1.0 + e2)
    w2 = e2 * w1
    w = jnp.where(lane == 0, w1, jnp.where(lane == 1, w2, 0.0))
    ii = jnp.where(lane == 0, i1, jnp.where(lane == 1, i2, 0.0)).astype(jnp.int32)
    wt_ref[...] = w[:, :8]
    idx_ref[...] = ii[:, :8]


def _router(h, g, sh, sc, rw, rb, S, tm, n_e):
    T, D = h.shape
    per_b = S // tm
    return pl.pallas_call(
        functools.partial(_router_kernel, n_e=n_e),
        grid=(T // tm,),
        in_specs=[
            pl.BlockSpec((tm, D), lambda i: (i, 0)),
            pl.BlockSpec((1, D), lambda i: (0, 0)),
            pl.BlockSpec((None, 1, D), lambda i: (i // per_b, 0, 0)),
            pl.BlockSpec((None, 1, D), lambda i: (i // per_b, 0, 0)),
            pl.BlockSpec((D, LANES), lambda i: (0, 0)),
            pl.BlockSpec((1, LANES), lambda i: (0, 0)),
        ],
        out_specs=[
            pl.BlockSpec((tm, D), lambda i: (i, 0)),
            pl.BlockSpec((tm, 8), lambda i: (i, 0)),
            pl.BlockSpec((tm, 8), lambda i: (i, 0)),
        ],
        out_shape=[
            jax.ShapeDtypeStruct((T, D), BF16),
            jax.ShapeDtypeStruct((T, 8), jnp.int32),
            jax.ShapeDtypeStruct((T, 8), F32),
        ],
        compiler_params=_params("parallel"),
        name="router",
    )(h, g, sh, sc, rw, rb)


def _forget_kernel(h_ref, g_ref, sh_ref, sc_ref, wf_ref, bf_ref, o_ref, carry_ref, *, per_b):
    @pl.when(pl.program_id(0) % per_b == 0)
    def _():
        carry_ref[...] = jnp.zeros_like(carry_ref)

    a = _norm_mod(h_ref[...], g_ref[...], sh_ref[...], sc_ref[...]).astype(BF16)
    z = jnp.dot(a, wf_ref[...], preferred_element_type=F32) + bf_ref[...]
    lf = jnp.minimum(z, 0.0) - jnp.log(1.0 + jnp.exp(-jnp.abs(z)))
    tm = lf.shape[0]
    r = lax.broadcasted_iota(jnp.int32, (tm, tm), 0)
    c = lax.broadcasted_iota(jnp.int32, (tm, tm), 1)
    tri = jnp.where(c <= r, 1.0, 0.0).astype(F32)
    cs = jnp.dot(tri, lf, preferred_element_type=F32, precision=lax.Precision.HIGHEST)
    out = cs + carry_ref[...]
    carry_ref[...] = out[tm - 1:tm, :]
    out2 = out * LOG2E
    hi = out2.astype(BF16)
    r1 = out2 - hi.astype(F32)
    mid = r1.astype(BF16)
    lo = (r1 - mid.astype(F32)).astype(BF16)
    o_ref[:, 0:LANES] = hi
    o_ref[:, LANES:2 * LANES] = mid
    o_ref[:, 2 * LANES:3 * LANES] = lo


def _forget_cumsum(h, g, sh, sc, wf, bf, S, tm):
    T, D = h.shape
    per_b = S // tm
    return pl.pallas_call(
        functools.partial(_forget_kernel, per_b=per_b),
        grid=(T // tm,),
        in_specs=[
            pl.BlockSpec((tm, D), lambda i: (i, 0)),
            pl.BlockSpec((1, D), lambda i: (0, 0)),
            pl.BlockSpec((None, 1, D), lambda i: (i // per_b, 0, 0)),
            pl.BlockSpec((None, 1, D), lambda i: (i // per_b, 0, 0)),
            pl.BlockSpec((D, LANES), lambda i: (0, 0)),
            pl.BlockSpec((1, LANES), lambda i: (0, 0)),
        ],
        out_specs=pl.BlockSpec((tm, 3 * LANES), lambda i: (i, 0)),
        out_shape=jax.ShapeDtypeStruct((T, 3 * LANES), BF16),
        scratch_shapes=[pltpu.VMEM((1, LANES), F32)],
        compiler_params=_params("arbitrary"),
        name="forget_cumsum",
    )(h, g, sh, sc, wf, bf)


def _fox_prep_kernel(kp_ref, vp_ref, f_ref, pm_ref, ka_ref, vt_ref, kn_ref, *, dh, tk):
    c = pl.program_id(2)
    n_sub = kp_ref.shape[0] // tk
    lane = lax.broadcasted_iota(jnp.int32, (tk, LANES), 1)
    lane8 = lax.broadcasted_iota(jnp.int32, (8, LANES), 1)
    row8 = lax.broadcasted_iota(jnp.int32, (8, LANES), 0)
    row16 = lax.broadcasted_iota(jnp.int32, (DV_ROWS - dh, tk), 0)
    tail = jnp.where(row16 == 0, 1.0, 0.0).astype(BF16)

    @pl.when(c == 0)
    def _():
        kn_ref[...] = jnp.zeros_like(kn_ref)

    for j in range(n_sub):
        rows = slice(j * tk, (j + 1) * tk)
        kp = kp_ref[rows, :]
        ex = jnp.dot(f_ref[rows, :], pm_ref[...], preferred_element_type=F32)
        ex = jnp.where((lane % dh) < 3, 1.0, ex).astype(BF16)
        ka_ref[0, rows, :] = jnp.where(lane < dh, kp, ex)
        ka_ref[1, rows, :] = jnp.where(lane < dh, ex, kp)
        vt = vp_ref[rows, :].astype(F32).T
        vt_ref[0, 0:dh, rows] = vt[0:dh].astype(BF16)
        vt_ref[1, 0:dh, rows] = vt[dh:2 * dh].astype(BF16)
        vt_ref[0, dh:DV_ROWS, rows] = tail
        vt_ref[1, dh:DV_ROWS, rows] = tail
        k2 = kp.astype(F32)
        k2 = k2 * k2
        na = jnp.max(jnp.sum(jnp.where(lane < dh, k2, 0.0), axis=1, keepdims=True), axis=0, keepdims=True)
        nb = jnp.max(jnp.sum(jnp.where(lane < dh, 0.0, k2), axis=1, keepdims=True), axis=0, keepdims=True)
        upd = jnp.where(row8 == 0, na, jnp.where(row8 == 1, nb, 0.0))
        kn_ref[...] = jnp.where(lane8 == c * n_sub + j, upd, kn_ref[...])


def _fox_prep(kv, fparts, pm, B, S, D, dh, tk, tp):
    H = D // dh
    nb = S // tp
    assert S // tk <= LANES and tp % tk == 0
    return pl.pallas_call(
        functools.partial(_fox_prep_kernel, dh=dh, tk=tk),
        grid=(B, H // 2, nb),
        in_specs=[
            pl.BlockSpec((tp, LANES), lambda b, p, c: (b * nb + c, p)),
            pl.BlockSpec((tp, LANES), lambda b, p, c: (b * nb + c, D // LANES + p)),
            pl.BlockSpec((tp, 3 * LANES), lambda b, p, c: (b * nb + c, 0)),
            pl.BlockSpec((None, 3 * LANES, LANES), lambda b, p, c: (p, 0, 0)),
        ],
        out_specs=[
            pl.BlockSpec((None, 2, tp, LANES), lambda b, p, c: (b, p, c, 0)),
            pl.BlockSpec((None, 2, DV_ROWS, tp), lambda b, p, c: (b, p, 0, c)),
            pl.BlockSpec((None, None, 8, LANES), lambda b, p, c: (b, p, 0, 0)),
        ],
        out_shape=[
            jax.ShapeDtypeStruct((B, H, S, LANES), BF16),
            jax.ShapeDtypeStruct((B, H, DV_ROWS, S), BF16),
            jax.ShapeDtypeStruct((B, H // 2, 8, LANES), F32),
        ],
        compiler_params=_params("parallel", "parallel", "arbitrary"),
        name="fox_prep",
    )(kv, kv, fparts, pm)


def _fox_kernel(q_ref, gq_ref, gb_ref, kn_ref, ka_ref, vt_ref, o_ref, sa0, sa1, sb0, sb1, *,
                tq, tk, dh):
    i = pl.program_id(2)
    qT = q_ref[...].T
    g16 = gq_ref[...]
    row8 = lax.broadcasted_iota(jnp.int32, (8, tq), 0)
    xa = jnp.where(row8 < 3, g16[0:8], jnp.where(row8 < 6, 1.0, 0.0))
    xb = jnp.where(row8 < 3, g16[8:16], jnp.where(row8 < 6, 1.0, 0.0))
    zpad = jnp.zeros((LANES - dh - 8, tq), F32)
    qa = jnp.concatenate([qT[0:dh], xa, zpad], axis=0).astype(BF16)
    qb = jnp.concatenate([xb, zpad, qT[dh:2 * dh]], axis=0).astype(BF16)

    def scores(h, q, blk):
        k0 = pl.multiple_of(blk * tk, tk)
        return jnp.dot(ka_ref[h, pl.ds(k0, tk), :], q, preferred_element_type=F32)

    def update(s_ref, h, blk, m, acc):
        k0 = pl.multiple_of(blk * tk, tk)
        s = s_ref[...]
        m_new = jnp.maximum(m, jnp.max(s, axis=0, keepdims=True))
        alpha = jnp.exp2(m - m_new)
        p = jnp.exp2(s - m_new).astype(BF16)
        pv = jnp.dot(vt_ref[h, :, pl.ds(k0, tk)], p, preferred_element_type=F32)
        return m_new, alpha * acc + pv

    q2 = qT * qT
    qna = jnp.max(jnp.sum(q2[0:dh], axis=0, keepdims=True), axis=1, keepdims=True)
    qnb = jnp.max(jnp.sum(q2[dh:2 * dh], axis=0, keepdims=True), axis=1, keepdims=True)
    kn = kn_ref[...]
    gb = gb_ref[...]
    lane = lax.broadcasted_iota(jnp.int32, (1, LANES), 1)
    at_i = lane == i

    def pick(row):
        return jnp.sum(jnp.where(at_i, row, 0.0), axis=1, keepdims=True)

    def needed(qn, knr, glast, gfirst):
        bound = NORM_SLACK * jnp.sqrt(qn) * (jnp.sqrt(knr) + jnp.sqrt(pick(knr))) + pick(gfirst) - glast
        return bound > -SKIP_LOG2

    need = jnp.logical_or(needed(qna, kn[0:1], gb[0:1], gb[2:3]), needed(qnb, kn[1:2], gb[1:2], gb[3:4]))
    lane_f = lane.astype(F32)
    i_f = i.astype(F32)
    cand = jnp.where(jnp.logical_and(need, lane < i), lane_f, i_f)
    jmin = jnp.min(cand).astype(jnp.int32)

    key = lax.broadcasted_iota(jnp.int32, (tk, tq), 0)
    qry = lax.broadcasted_iota(jnp.int32, (tk, tq), 1)
    sa0[...] = jnp.where(key <= qry, scores(0, qa, i), NEG_BIG)
    sb0[...] = jnp.where(key <= qry, scores(1, qb, i), NEG_BIG)
    ma0 = jnp.max(sa0[...], axis=0, keepdims=True)
    mb0 = jnp.max(sb0[...], axis=0, keepdims=True)

    def first(s_ref, h, m0):
        k0 = pl.multiple_of(i * tk, tk)
        p = jnp.exp2(s_ref[...] - m0).astype(BF16)
        return jnp.dot(vt_ref[h, :, pl.ds(k0, tk)], p, preferred_element_type=F32)

    acca0 = first(sa0, 0, ma0)
    accb0 = first(sb0, 1, mb0)

    n_off = i - jmin
    n_pairs = jnp.maximum((n_off + 1) // 2, 1)

    def blk_at(pos):
        return jnp.minimum(jmin + pos, i)

    def masked_scores(h, q, pos):
        return jnp.where(pos < n_off, scores(h, q, blk_at(pos)), NEG_BIG)

    sa0[...] = masked_scores(0, qa, 0)
    sb0[...] = masked_scores(1, qb, 0)

    def pair(u, carry):
        ma, acca, mb, accb = carry
        b0 = jmin + 2 * u
        sa1[...] = scores(0, qa, b0 + 1)
        sb1[...] = scores(1, qb, b0 + 1)
        ma, acca = update(sa0, 0, b0, ma, acca)
        mb, accb = update(sb0, 1, b0, mb, accb)
        sa0[...] = scores(0, qa, b0 + 2)
        sb0[...] = scores(1, qb, b0 + 2)
        ma, acca = update(sa1, 0, b0 + 1, ma, acca)
        mb, accb = update(sb1, 1, b0 + 1, mb, accb)
        return ma, acca, mb, accb

    carry = lax.fori_loop(0, n_pairs - 1, pair, (ma0, acca0, mb0, accb0))
    u = n_pairs - 1

    def tail_two(carry):
        ma, acca, mb, accb = carry
        sa1[...] = scores(0, qa, blk_at(2 * u + 1))
        sb1[...] = scores(1, qb, blk_at(2 * u + 1))
        ma, acca = update(sa0, 0, blk_at(2 * u), ma, acca)
        mb, accb = update(sb0, 1, blk_at(2 * u), mb, accb)
        ma, acca = update(sa1, 0, blk_at(2 * u + 1), ma, acca)
        mb, accb = update(sb1, 1, blk_at(2 * u + 1), mb, accb)
        return ma, acca, mb, accb

    def tail_one(carry):
        ma, acca, mb, accb = carry
        ma, acca = update(sa0, 0, blk_at(2 * u), ma, acca)
        mb, accb = update(sb0, 1, blk_at(2 * u), mb, accb)
        return ma, acca, mb, accb

    ma, acca, mb, accb = lax.cond(2 * u + 1 < n_off, tail_two, tail_one, carry)
    oa = acca[:dh, :] * (1.0 / acca[dh:dh + 1, :])
    ob = accb[:dh, :] * (1.0 / accb[dh:dh + 1, :])
    o_ref[...] = jnp.concatenate([oa, ob], axis=0).T


def _fox_attention(qg, gq, gb, kn, ka, vt, B, S, D, dh, tq):
    T = B * S
    H = D // dh
    nq = S // tq
    return pl.pallas_call(
        functools.partial(_fox_kernel, tq=tq, tk=tq, dh=dh),
        grid=(B, H // 2, nq),
        in_specs=[
            pl.BlockSpec((tq, LANES), lambda b, p, i: (b * nq + i, p)),
            pl.BlockSpec((None, None, 16, tq), lambda b, p, i: (b, p, 0, i)),
            pl.BlockSpec((None, None, 8, LANES), lambda b, p, i: (b, p, 0, 0)),
            pl.BlockSpec((None, None, 8, LANES), lambda b, p, i: (b, p, 0, 0)),
            pl.BlockSpec((None, 2, S, LANES), lambda b, p, i: (b, p, 0, 0)),
            pl.BlockSpec((None, 2, DV_ROWS, S), lambda b, p, i: (b, p, 0, 0)),
        ],
        out_specs=pl.BlockSpec((tq, LANES), lambda b, p, i: (b * nq + i, p)),
        out_shape=jax.ShapeDtypeStruct((T, D), F32),
        scratch_shapes=[pltpu.VMEM((tq, tq), F32) for _ in range(4)],
        compiler_params=_params("parallel", "parallel", "arbitrary"),
        name="fox_attention",
    )(qg, gq, gb, kn, ka, vt)


def _fox_gate_tables(fparts, B, S, H, tk):
    f3 = fparts.reshape(B, S, 3, LANES)[:, :, :, :H].astype(F32)
    gq = f3.reshape(B, S, 3, H // 2, 2).transpose(0, 3, 4, 2, 1)
    gq = jnp.pad(gq, ((0, 0), (0, 0), (0, 0), (0, 5), (0, 0))).reshape(B, H // 2, 16, S)
    G = jnp.sum(f3, axis=2)
    nb = S // tk
    Gb = G.reshape(B, nb, tk, H // 2, 2)
    last = Gb[:, :, tk - 1].transpose(0, 2, 3, 1)
    first = Gb[:, :, 0].transpose(0, 2, 3, 1)
    gb = jnp.concatenate([last, first, jnp.zeros_like(last), jnp.zeros_like(last)], axis=2)
    gb = jnp.pad(gb, ((0, 0), (0, 0), (0, 0), (0, LANES - nb)))
    return gq, gb


def _fox_placement(H, dh):
    p = jnp.arange(H // 2)[:, None, None]
    r = jnp.arange(3 * LANES)[None, :, None]
    l = jnp.arange(LANES)[None, None, :]
    j = r // LANES
    head = r % LANES
    hit_b = jnp.logical_and(head == 2 * p + 1, l == 3 + j)
    hit_a = jnp.logical_and(head == 2 * p, l == dh + 3 + j)
    return jnp.where(jnp.logical_or(hit_a, hit_b), -1.0, 0.0).astype(BF16)


def _final_kernel(h_ref, g_ref, sh_ref, sc_ref, o_ref):
    o_ref[...] = _norm_mod(h_ref[...], g_ref[...], sh_ref[...], sc_ref[...])


def _final_norm(h, g, sh, sc, S, tm):
    T, D = h.shape
    per_b = S // tm
    return pl.pallas_call(
        _final_kernel,
        grid=(T // tm,),
        in_specs=[
            pl.BlockSpec((tm, D), lambda i: (i, 0)),
            pl.BlockSpec((1, D), lambda i: (0, 0)),
            pl.BlockSpec((None, 1, D), lambda i: (i // per_b, 0, 0)),
            pl.BlockSpec((None, 1, D), lambda i: (i // per_b, 0, 0)),
        ],
        out_specs=pl.BlockSpec((tm, D), lambda i: (i, 0)),
        out_shape=jax.ShapeDtypeStruct((T, D), F32),
        compiler_params=_params("parallel"),
        name="final_norm",
    )(h, g, sh, sc)


def _dispatch_tables(top_idx, n_e, tm):
    T, K = top_idx.shape
    n = T * K
    n_tiles = n // tm
    n_units = n_tiles + n_e
    e_flat = top_idx.reshape(n)
    order = jnp.argsort(e_flat, stable=True).astype(jnp.int32)
    pos = jnp.argsort(order).astype(jnp.int32)
    ids = jnp.arange(n_e, dtype=jnp.int32)
    counts = jnp.sum(e_flat[:, None] == ids[None, :], axis=0).astype(jnp.int32)
    ends = jnp.cumsum(counts)
    starts = ends - counts
    t0 = jnp.arange(n_tiles, dtype=jnp.int32)[:, None] * tm
    present = jnp.logical_and(starts[None, :] < t0 + tm, ends[None, :] > t0)
    present = jnp.logical_and(present, counts[None, :] > 0)
    flat = jnp.arange(n_tiles * n_e, dtype=jnp.int32)
    keys = jnp.sort(jnp.where(present.reshape(-1), flat, n_tiles * n_e))[:n_units]
    valid = keys < n_tiles * n_e
    n_valid = jnp.sum(valid.astype(jnp.int32))
    keys = jnp.where(valid, keys, keys[jnp.maximum(n_valid - 1, 0)])
    ut = keys // n_e
    ue = keys % n_e
    ulo = jnp.clip(starts[ue] - ut * tm, 0, tm)
    uhi = jnp.clip(ends[ue] - ut * tm, 0, tm)
    prev_t = jnp.concatenate([jnp.full((1,), -1, jnp.int32), ut[:-1]])
    ufirst = (ut != prev_t).astype(jnp.int32)
    units = tuple(a.astype(jnp.int32) for a in (ut, ue, ulo, uhi, ufirst, valid))
    return order, pos.reshape(T, K), units


def _tile(S, want):
    t = min(S, want)
    assert S % t == 0
    return t


def _lane_tile(F, cap):
    return max(t for t in range(LANES, min(F, cap) + 1, LANES) if F % t == 0)


def kernel(x, c, ada_w, ada_b, norm_g, ret_w_in, ret_w_o, kv_ada_w, kv_ada_b, kv_norm_g, fox_w_kv, fox_w_f, fox_b_f, fox_w_qg, fox_w_o, ffn_w_gate, ffn_w_up, ffn_w_down, router_w, router_b, moe_w_gate, moe_w_up, moe_w_down, final_ada_w, final_ada_b, final_norm_g):
    B, S, D = x.shape
    T = B * S
    depth = ada_w.shape[0]
    n_a = ret_w_in.shape[0]
    n_e = router_w.shape[-1]
    dh = D // FOX_HEADS
    dk = D // RET_HEADS
    assert dk == 2 * LANES and S % CHUNK == 0 and n_e <= 8

    tm = _tile(S, 512)
    tm_ffn = _tile(S, 1024)
    tm_moe = _tile(S, 1024)
    tq = _tile(S, 512)
    lc = _tile(S, 512)
    f_dense = ffn_w_gate.shape[-1]
    f_moe = moe_w_gate.shape[-1]
    tf_dense = f_dense // 2 if (f_dense // 2) % LANES == 0 else f_dense
    tf_moe = _lane_tile(f_moe, 1024)

    c_pad = jnp.zeros((8, D), F32).at[:B].set(c)
    ada = _ada(c_pad, ada_w, ada_b[:, None, :])[:, :B]
    extra_w = jnp.stack([kv_ada_w, final_ada_w])
    extra_b = jnp.stack([kv_ada_b, final_ada_b])[:, None, :]
    extra = _ada(c_pad, extra_w, extra_b)[:, :B]
    vec = lambda a: a[:, None, :]

    h = x.reshape(T, D)
    cos, sin = _rotary_tables(S, dk)
    ret_tabs = _retention_tables()
    shared = None
    w_in_b, w_ro_b = ret_w_in.astype(BF16), ret_w_o.astype(BF16)
    w_kv_b, w_qg_b, w_fo_b = fox_w_kv.astype(BF16)[None], fox_w_qg.astype(BF16), fox_w_o.astype(BF16)
    w_fg_b, w_fu_b, w_fd_b = ffn_w_gate.astype(BF16), ffn_w_up.astype(BF16), ffn_w_down.astype(BF16)
    w_mg_b, w_mu_b, w_md_b = moe_w_gate.astype(BF16), moe_w_up.astype(BF16), moe_w_down.astype(BF16)

    f_sh, f_sc = [vec(a) for a in jnp.split(extra[1], 2, axis=-1)]
    final = (final_norm_g[None, :], f_sh, f_sc)

    for l in range(depth):
        sh1, sc1, g1, sh2, sc2, g2 = [vec(a) for a in jnp.split(ada[l], 6, axis=-1)]
        ng1 = norm_g[l, 0][None, :]
        ng2 = norm_g[l, 1][None, :]
        if l == n_a:
            kv_sh, kv_sc = [vec(a) for a in jnp.split(extra[0], 2, axis=-1)]
            kvg = kv_norm_g[None, :]
            kv = _norm_mod_matmul(h, kvg, kv_sh, kv_sc, w_kv_b, S,
                                  modes=("plain",) * 4, scales=(1.0,) * 4, cw=D // 2,
                                  out_dtype=BF16, tm=tm)
            wf = jnp.zeros((D, LANES), F32).at[:, :FOX_HEADS].set(fox_w_f).astype(BF16)
            bf = jnp.zeros((1, LANES), F32).at[0, :FOX_HEADS].set(fox_b_f)
            fparts = _forget_cumsum(h, kvg, kv_sh, kv_sc, wf, bf, S, tm)
            ka, vt, kn = _fox_prep(kv, fparts, _fox_placement(FOX_HEADS, dh), B, S, D, dh, tq,
                                   _tile(S, 4 * tq))
            gq, gb = _fox_gate_tables(fparts, B, S, FOX_HEADS, tq)
            shared = (gq, gb, kn, ka, vt)
        if l < n_a:
            qkv = _norm_mod_matmul(
                h, ng1, sh1, sc1, w_in_b, S,
                modes=("rot",) * 4 + ("plain",) * 4,
                scales=(1.0,) * 2 + (dk ** -0.5,) * 2 + (1.0,) * 4,
                cw=D // 2, out_dtype=BF16, tm=tm, rot=(cos, sin), w_layer=l, w_col=0)
            gate = _norm_mod_matmul(h, ng1, sh1, sc1, w_in_b, S,
                                    modes=("plain",) * 4, scales=(1.0,) * 4, cw=D // 2,
                                    out_dtype=F32, tm=tm, w_layer=l, w_col=2)
            y = _retention(qkv, gate, ret_tabs, B, S, lc)
            h = _proj_residual(y, w_ro_b, l, h, g1, S, tm)
        else:
            j = l - n_a
            qg = _norm_mod_matmul(h, ng1, sh1, sc1, w_qg_b, S,
                                  modes=("plain",) * 4, scales=(dh ** -0.5 * LOG2E,) * 2 + (1.0,) * 2,
                                  cw=D // 2, out_dtype=F32, tm=tm, w_layer=j)
            o = _fox_attention(qg, *shared, B, S, D, dh, tq)
            h = _proj_residual(o, w_fo_b, j, h, g1, S, tm, og=qg, og_block=1)
        i = l // 2
        if l % 2 == 0:
            h = _ffn_dense(h, ng2, sh2, sc2, g2, w_fg_b, w_fu_b, w_fd_b, i, S, tm_ffn, tf_dense)
        else:
            rw = jnp.zeros((D, LANES), F32).at[:, :n_e].set(router_w[i]).astype(BF16)
            rb = jnp.zeros((1, LANES), F32).at[0, :n_e].set(router_b[i])
            m, idx8, wt8 = _router(h, ng2, sh2, sc2, rw, rb, S, tm, n_e)
            order, pos, units = _dispatch_tables(idx8[:, :TOP_K], n_e, tm_moe)
            xs = m.at[order // TOP_K].get(mode="promise_in_bounds")
            ys = _ffn_moe(units, xs, w_mg_b, w_mu_b, w_md_b, i, tm_moe, tf_moe)
            ya = ys.at[pos[:, 0]].get(mode="promise_in_bounds")
            yb = ys.at[pos[:, 1]].get(mode="promise_in_bounds")
            h = _moe_combine(h, ya, yb, wt8, g2, S, tm, final=final if l == depth - 1 else None)

    if (depth - 1) % 2 == 0:
        h = _final_norm(h, *final, S, tm)
    return h.reshape(B, S, D)
```

```python
import functools
import math

import jax
import jax.numpy as jnp
from jax import lax
from jax.experimental import pallas as pl
from jax.experimental.pallas import tpu as pltpu

F32 = jnp.float32
BF16 = jnp.bfloat16

CHUNK = 64
RET_HEADS = 4
FOX_HEADS = 16
ROPE_BASE = 10000.0
TOP_K = 2
EPS = 1e-6

LANES = 128
MXU_DIM = 256
VMEM_LIMIT_BYTES = 56 * 1024 * 1024
NEG_BIG = -1e30
DV_ROWS = 80
SKIP_LOG2 = 160.0
NORM_SLACK = 1.02
LOG2E = math.log2(math.e)


def _params(*sem):
    return pltpu.CompilerParams(dimension_semantics=sem, vmem_limit_bytes=VMEM_LIMIT_BYTES)


def _silu(x):
    return x * (1.0 / (1.0 + jnp.exp(-x)))


def _sigmoid(x):
    return 1.0 / (1.0 + jnp.exp(-x))


def _norm_mod(x, g, sh, sc):
    var = jnp.mean(x * x, axis=-1, keepdims=True)
    y = (x * lax.rsqrt(var + EPS)) * g
    return y * (1.0 + sc) + sh


def _ada_kernel(c_ref, w_ref, b_ref, o_ref):
    ca = _silu(c_ref[...]).astype(BF16)
    o_ref[...] = jnp.dot(ca, w_ref[...].astype(BF16), preferred_element_type=F32) + b_ref[...]


def _ada(c_pad, w, b, tn=1024):
    L, D, N = w.shape
    return pl.pallas_call(
        _ada_kernel,
        grid=(L, N // tn),
        in_specs=[
            pl.BlockSpec((8, D), lambda l, j: (0, 0)),
            pl.BlockSpec((None, D, tn), lambda l, j: (l, 0, j)),
            pl.BlockSpec((None, 1, tn), lambda l, j: (l, 0, j)),
        ],
        out_specs=pl.BlockSpec((None, 8, tn), lambda l, j: (l, 0, j)),
        out_shape=jax.ShapeDtypeStruct((L, 8, N), F32),
        compiler_params=_params("parallel", "parallel"),
        name="ada_proj",
    )(c_pad, w, b)


def _nmm_kernel(*refs, n_chunks, cw, modes, scales, out_dtype):
    h_ref, g_ref, sh_ref, sc_ref, w_ref = refs[:5]
    has_rot = any(m == "rot" for m in modes)
    if has_rot:
        cos_ref, sin_ref, o_ref = refs[5:8]
    else:
        o_ref = refs[5]
    a = _norm_mod(h_ref[...], g_ref[...], sh_ref[...], sc_ref[...]).astype(BF16)
    for c in range(n_chunks):
        acc = jnp.dot(a, w_ref[:, c * cw:(c + 1) * cw], preferred_element_type=F32)
        if modes[c] == "rot":
            cos = cos_ref[...]
            sin = sin_ref[...]
            parts = []
            for hh in range(cw // (2 * LANES)):
                x1 = acc[:, (2 * hh) * LANES:(2 * hh + 1) * LANES]
                x2 = acc[:, (2 * hh + 1) * LANES:(2 * hh + 2) * LANES]
                parts.append(x1 * cos - x2 * sin)
                parts.append(x2 * cos + x1 * sin)
            acc = jnp.concatenate(parts, axis=-1)
        if scales[c] != 1.0:
            acc = acc * scales[c]
        o_ref[:, c * cw:(c + 1) * cw] = acc.astype(out_dtype)


def _norm_mod_matmul(h, g, sh, sc, w, S, *, modes, scales, cw, out_dtype, tm, rot=None, w_layer=0,
                     w_col=0):
    T, D = h.shape
    n_chunks = len(modes)
    N = n_chunks * cw
    per_b = S // tm
    in_specs = [
        pl.BlockSpec((tm, D), lambda i: (i, 0)),
        pl.BlockSpec((1, D), lambda i: (0, 0)),
        pl.BlockSpec((None, 1, D), lambda i: (i // per_b, 0, 0)),
        pl.BlockSpec((None, 1, D), lambda i: (i // per_b, 0, 0)),
        pl.BlockSpec((None, D, N), lambda i: (w_layer, 0, w_col)),
    ]
    args = [h, g, sh, sc, w]
    if rot is not None:
        in_specs += [pl.BlockSpec((tm, LANES), lambda i: (i % per_b, 0))] * 2
        args += list(rot)
    return pl.pallas_call(
        functools.partial(_nmm_kernel, n_chunks=n_chunks, cw=cw, modes=modes, scales=scales,
                          out_dtype=out_dtype),
        grid=(T // tm,),
        in_specs=in_specs,
        out_specs=pl.BlockSpec((tm, N), lambda i: (i, 0)),
        out_shape=jax.ShapeDtypeStruct((T, N), out_dtype),
        compiler_params=_params("parallel"),
        name="norm_mod_matmul",
    )(*args)


def _ret_kernel(q_ref, k_ref, v_ref, g_ref, intra_ref, qd_ref, kd_ref, cd_ref, o_ref, st_ref, *,
                n_sub, dk, dv):
    @pl.when(pl.program_id(1) == 0)
    def _():
        st_ref[...] = jnp.zeros_like(st_ref)

    def body(c, carry):
        r0 = pl.multiple_of(c * CHUNK, CHUNK)
        rows = pl.ds(r0, CHUNK)
        for h in range(RET_HEADS):
            qc = q_ref[rows, h * dk:(h + 1) * dk]
            kc = k_ref[rows, h * dk:(h + 1) * dk]
            vc = v_ref[rows, h * dv:(h + 1) * dv]
            state = st_ref[h]
            scores = lax.dot_general(qc, kc, (((1,), (1,)), ((), ())),
                                     preferred_element_type=F32) * intra_ref[h]
            o = (jnp.dot(scores.astype(BF16), vc, preferred_element_type=F32)
                 + jnp.dot(qc, state.astype(BF16), preferred_element_type=F32) * qd_ref[h])
            kdk = (kc.astype(F32) * kd_ref[h]).astype(BF16)
            st_ref[h] = state * cd_ref[h] + lax.dot_general(kdk, vc, (((0,), (0,)), ((), ())),
                                                             preferred_element_type=F32)
            mu = jnp.mean(o, axis=-1, keepdims=True)
            d = o - mu
            var = jnp.mean(d * d, axis=-1, keepdims=True)
            y = d * lax.rsqrt(var + EPS)
            gate = g_ref[rows, h * dv:(h + 1) * dv]
            o_ref[rows, h * dv:(h + 1) * dv] = (_silu(gate) * y).astype(o_ref.dtype)
        return carry

    lax.fori_loop(0, n_sub, body, 0)


def _retention(qkv, gate, tabs, B, S, lc):
    T = qkv.shape[0]
    D = qkv.shape[1] // 4
    dk = D // RET_HEADS
    dv = 2 * D // RET_HEADS
    H = RET_HEADS
    per_b = S // lc
    intra, qd, kd, cd = tabs
    row = lambda b, s: b * per_b + s
    return pl.pallas_call(
        functools.partial(_ret_kernel, n_sub=lc // CHUNK, dk=dk, dv=dv),
        grid=(B, per_b),
        in_specs=[
            pl.BlockSpec((lc, D), lambda b, s: (row(b, s), 0)),
            pl.BlockSpec((lc, D), lambda b, s: (row(b, s), 1)),
            pl.BlockSpec((lc, 2 * D), lambda b, s: (row(b, s), 1)),
            pl.BlockSpec((lc, 2 * D), lambda b, s: (row(b, s), 0)),
            pl.BlockSpec((H, CHUNK, CHUNK), lambda b, s: (0, 0, 0)),
            pl.BlockSpec((H, CHUNK, 1), lambda b, s: (0, 0, 0)),
            pl.BlockSpec((H, CHUNK, 1), lambda b, s: (0, 0, 0)),
            pl.BlockSpec((H, 1, 1), lambda b, s: (0, 0, 0)),
        ],
        out_specs=pl.BlockSpec((lc, 2 * D), lambda b, s: (row(b, s), 0)),
        out_shape=jax.ShapeDtypeStruct((T, 2 * D), BF16),
        scratch_shapes=[pltpu.VMEM((H, dk, dv), F32)],
        compiler_params=_params("parallel", "arbitrary"),
        name="retention",
    )(qkv, qkv, qkv, gate, intra, qd, kd, cd)


def _retention_tables():
    h = jnp.arange(RET_HEADS, dtype=F32)
    log_g = jnp.log(1.0 - 2.0 ** (-5.0 - h))
    idx = jnp.arange(CHUNK, dtype=F32)
    intra = jnp.exp(log_g[:, None, None] * jnp.abs(idx[:, None] - idx[None, :]))
    qd = jnp.exp(log_g[:, None] * (idx[None, :] + 1.0))[:, :, None]
    kd = jnp.exp(log_g[:, None] * (CHUNK - 1.0 - idx[None, :]))[:, :, None]
    cd = jnp.exp(log_g * CHUNK)[:, None, None]
    return intra, qd, kd, cd


def _rotary_tables(S, d):
    inv = 1.0 / (ROPE_BASE ** (jnp.arange(0, d, 2, dtype=F32) / d))
    ang = jnp.arange(S).astype(F32)[:, None] * inv[None, :]
    return jnp.cos(ang), jnp.sin(ang)


def _proj_res_kernel(*refs, gated):
    if gated:
        o_in, og_ref, w_ref, h_ref, gt_ref, out_ref = refs
        x = (o_in[...] * _sigmoid(og_ref[...])).astype(BF16)
    else:
        x_ref, w_ref, h_ref, gt_ref, out_ref = refs
        x = x_ref[...]
    out_ref[...] = h_ref[...] + gt_ref[...] * jnp.dot(x, w_ref[...], preferred_element_type=F32)


def _proj_residual(x, w, w_layer, h, gate, S, tm, og=None, og_block=0):
    T, D = h.shape
    K = w.shape[1]
    per_b = S // tm
    gated = og is not None
    xs = (x, og) if gated else (x,)
    in_specs = [pl.BlockSpec((tm, K), lambda i: (i, 0))]
    if gated:
        in_specs.append(pl.BlockSpec((tm, K), lambda i: (i, og_block)))
    in_specs += [
        pl.BlockSpec((None, K, D), lambda i: (w_layer, 0, 0)),
        pl.BlockSpec((tm, D), lambda i: (i, 0)),
        pl.BlockSpec((None, 1, D), lambda i: (i // per_b, 0, 0)),
    ]
    return pl.pallas_call(
        functools.partial(_proj_res_kernel, gated=gated),
        grid=(T // tm,),
        in_specs=in_specs,
        out_specs=pl.BlockSpec((tm, D), lambda i: (i, 0)),
        out_shape=jax.ShapeDtypeStruct((T, D), F32),
        compiler_params=_params("parallel"),
        name="proj_residual",
    )(*xs, w, h, gate)


def _ffn_dense_kernel(h_ref, g_ref, sh_ref, sc_ref, gt_ref, wg_ref, wu_ref, wd_ref, o_ref, a_scr,
                      acc_scr):
    f = pl.program_id(1)

    @pl.when(f == 0)
    def _():
        a_scr[...] = _norm_mod(h_ref[...], g_ref[...], sh_ref[...], sc_ref[...]).astype(BF16)
        acc_scr[...] = jnp.zeros_like(acc_scr)

    a = a_scr[...]
    gate = jnp.dot(a, wg_ref[...], preferred_element_type=F32)
    up = jnp.dot(a, wu_ref[...], preferred_element_type=F32)
    act = (_silu(gate) * up).astype(BF16)
    acc_scr[...] += jnp.dot(act, wd_ref[...], preferred_element_type=F32)

    @pl.when(f == pl.num_programs(1) - 1)
    def _():
        o_ref[...] = h_ref[...] + gt_ref[...] * acc_scr[...]


def _ffn_dense(h, g, sh, sc, gate, wg, wu, wd, layer, S, tm, tf):
    T, D = h.shape
    F = wg.shape[2]
    per_b = S // tm
    return pl.pallas_call(
        _ffn_dense_kernel,
        grid=(T // tm, F // tf),
        in_specs=[
            pl.BlockSpec((tm, D), lambda i, f: (i, 0)),
            pl.BlockSpec((1, D), lambda i, f: (0, 0)),
            pl.BlockSpec((None, 1, D), lambda i, f: (i // per_b, 0, 0)),
            pl.BlockSpec((None, 1, D), lambda i, f: (i // per_b, 0, 0)),
            pl.BlockSpec((None, 1, D), lambda i, f: (i // per_b, 0, 0)),
            pl.BlockSpec((None, D, tf), lambda i, f: (layer, 0, f)),
            pl.BlockSpec((None, D, tf), lambda i, f: (layer, 0, f)),
            pl.BlockSpec((None, tf, D), lambda i, f: (layer, f, 0)),
        ],
        out_specs=pl.BlockSpec((tm, D), lambda i, f: (i, 0)),
        out_shape=jax.ShapeDtypeStruct((T, D), F32),
        scratch_shapes=[pltpu.VMEM((tm, D), BF16), pltpu.VMEM((tm, D), F32)],
        compiler_params=_params("parallel", "arbitrary"),
        name="ffn_dense",
    )(h, g, sh, sc, gate, wg, wu, wd)


def _ffn_moe_kernel(ut_ref, ue_ref, ulo_ref, uhi_ref, ufirst_ref, uvalid_ref, x_ref, wg_ref, wu_ref,
                    wd_ref, o_ref, acc_scr):
    u = pl.program_id(0)
    f = pl.program_id(1)
    valid = uvalid_ref[u] > 0

    @pl.when(jnp.logical_and(jnp.logical_and(valid, ufirst_ref[u] > 0), f == 0))
    def _():
        acc_scr[...] = jnp.zeros_like(acc_scr)

    @pl.when(valid)
    def _():
        x = x_ref[...]
        gate = jnp.dot(x, wg_ref[...].astype(BF16), preferred_element_type=F32)
        up = jnp.dot(x, wu_ref[...].astype(BF16), preferred_element_type=F32)
        row = lax.broadcasted_iota(jnp.int32, gate.shape, 0)
        mine = jnp.logical_and(row >= ulo_ref[u], row < uhi_ref[u])
        act = jnp.where(mine, _silu(gate) * up, 0.0).astype(BF16)
        acc_scr[...] += jnp.dot(act, wd_ref[...].astype(BF16), preferred_element_type=F32)

    @pl.when(jnp.logical_and(valid, f == pl.num_programs(1) - 1))
    def _():
        o_ref[...] = acc_scr[...]


def _ffn_moe(units, xs, wg, wu, wd, layer, tm, tf):
    N, D = xs.shape
    F = wg.shape[3]
    nf = F // tf
    n_units = units[0].shape[0]

    def xrow(u, f, ut, ue, ulo, uhi, ufirst, uvalid):
        return (ut[u], 0)

    def wcol(u, f, ut, ue, ulo, uhi, ufirst, uvalid):
        return (layer, ue[u], 0, jnp.where(uvalid[u] > 0, f, nf - 1))

    def wrow(u, f, ut, ue, ulo, uhi, ufirst, uvalid):
        return (layer, ue[u], jnp.where(uvalid[u] > 0, f, nf - 1), 0)

    grid_spec = pltpu.PrefetchScalarGridSpec(
        num_scalar_prefetch=6,
        grid=(n_units, nf),
        in_specs=[
            pl.BlockSpec((tm, D), xrow),
            pl.BlockSpec((None, None, D, tf), wcol),
            pl.BlockSpec((None, None, D, tf), wcol),
            pl.BlockSpec((None, None, tf, D), wrow),
        ],
        out_specs=pl.BlockSpec((tm, D), xrow),
        scratch_shapes=[pltpu.VMEM((tm, D), F32)],
    )
    return pl.pallas_call(
        _ffn_moe_kernel,
        grid_spec=grid_spec,
        out_shape=jax.ShapeDtypeStruct((N, D), F32),
        compiler_params=_params("arbitrary", "arbitrary"),
        name="ffn_moe",
    )(*units, xs, wg, wu, wd)


def _combine_kernel(*refs, fused_final):
    h_ref, ya_ref, yb_ref, wt_ref, gt_ref = refs[:5]
    o_ref = refs[-1]
    wt = wt_ref[...]
    ff = wt[:, 0:1] * ya_ref[...] + wt[:, 1:2] * yb_ref[...]
    out = h_ref[...] + gt_ref[...] * ff
    if fused_final:
        g_ref, sh_ref, sc_ref = refs[5:8]
        out = _norm_mod(out, g_ref[...], sh_ref[...], sc_ref[...])
    o_ref[...] = out


def _moe_combine(h, ya, yb, wt8, gate, S, tm, final=None):
    T, D = h.shape
    per_b = S // tm
    row = pl.BlockSpec((tm, D), lambda i: (i, 0))
    per_batch = pl.BlockSpec((None, 1, D), lambda i: (i // per_b, 0, 0))
    in_specs = [row, row, row, pl.BlockSpec((tm, 8), lambda i: (i, 0)), per_batch]
    args = [h, ya, yb, wt8, gate]
    if final is not None:
        in_specs += [pl.BlockSpec((1, D), lambda i: (0, 0)), per_batch, per_batch]
        args += list(final)
    return pl.pallas_call(
        functools.partial(_combine_kernel, fused_final=final is not None),
        grid=(T // tm,),
        in_specs=in_specs,
        out_specs=row,
        out_shape=jax.ShapeDtypeStruct((T, D), F32),
        compiler_params=_params("parallel"),
        name="moe_combine",
    )(*args)


def _router_kernel(h_ref, g_ref, sh_ref, sc_ref, rw_ref, rb_ref, m_ref, idx_ref, wt_ref, *, n_e):
    a = _norm_mod(h_ref[...], g_ref[...], sh_ref[...], sc_ref[...]).astype(BF16)
    m_ref[...] = a
    logits = jnp.dot(a, rw_ref[...], preferred_element_type=F32) + rb_ref[...]
    lane = lax.broadcasted_iota(jnp.int32, logits.shape, 1)
    lane_f = lane.astype(F32)
    logits = jnp.where(lane < n_e, logits, NEG_BIG)
    m1 = jnp.max(logits, axis=-1, keepdims=True)
    i1 = jnp.min(jnp.where(logits == m1, lane_f, float(LANES)), axis=-1, keepdims=True)
    rest = jnp.where(lane_f == i1, NEG_BIG, logits)
    m2 = jnp.max(rest, axis=-1, keepdims=True)
    i2 = jnp.min(jnp.where(rest == m2, lane_f, float(LANES)), axis=-1, keepdims=True)
    e2 = jnp.exp(m2 - m1)
    w1 = 1.0 / (1.0 + e2)
    w2 = e2 * w1
    w = jnp.where(lane == 0, w1, jnp.where(lane == 1, w2, 0.0))
    ii = jnp.where(lane == 0, i1, jnp.where(lane == 1, i2, 0.0)).astype(jnp.int32)
    wt_ref[...] = w[:, :8]
    idx_ref[...] = ii[:, :8]


def _router(h, g, sh, sc, rw, rb, S, tm, n_e):
    T, D = h.shape
    per_b = S // tm
    return pl.pallas_call(
        functools.partial(_router_kernel, n_e=n_e),
        grid=(T // tm,),
        in_specs=[
            pl.BlockSpec((tm, D), lambda i: (i, 0)),
            pl.BlockSpec((1, D), lambda i: (0, 0)),
            pl.BlockSpec((None, 1, D), lambda i: (i // per_b, 0, 0)),
            pl.BlockSpec((None, 1, D), lambda i: (i // per_b, 0, 0)),
            pl.BlockSpec((D, LANES), lambda i: (0, 0)),
            pl.BlockSpec((1, LANES), lambda i: (0, 0)),
        ],
        out_specs=[
            pl.BlockSpec((tm, D), lambda i: (i, 0)),
            pl.BlockSpec((tm, 8), lambda i: (i, 0)),
            pl.BlockSpec((tm, 8), lambda i: (i, 0)),
        ],
        out_shape=[
            jax.ShapeDtypeStruct((T, D), BF16),
            jax.ShapeDtypeStruct((T, 8), jnp.int32),
            jax.ShapeDtypeStruct((T, 8), F32),
        ],
        compiler_params=_params("parallel"),
        name="router",
    )(h, g, sh, sc, rw, rb)


def _forget_kernel(h_ref, g_ref, sh_ref, sc_ref, wf_ref, bf_ref, o_ref, carry_ref, *, per_b):
    @pl.when(pl.program_id(0) % per_b == 0)
    def _():
        carry_ref[...] = jnp.zeros_like(carry_ref)

    a = _norm_mod(h_ref[...], g_ref[...], sh_ref[...], sc_ref[...]).astype(BF16)
    z = jnp.dot(a, wf_ref[...], preferred_element_type=F32) + bf_ref[...]
    lf = jnp.minimum(z, 0.0) - jnp.log(1.0 + jnp.exp(-jnp.abs(z)))
    tm = lf.shape[0]
    r = lax.broadcasted_iota(jnp.int32, (tm, tm), 0)
    c = lax.broadcasted_iota(jnp.int32, (tm, tm), 1)
    tri = jnp.where(c <= r, 1.0, 0.0).astype(F32)
    cs = jnp.dot(tri, lf, preferred_element_type=F32, precision=lax.Precision.HIGHEST)
    out = cs + carry_ref[...]
    carry_ref[...] = out[tm - 1:tm, :]
    out2 = out * LOG2E
    hi = out2.astype(BF16)
    r1 = out2 - hi.astype(F32)
    mid = r1.astype(BF16)
    lo = (r1 - mid.astype(F32)).astype(BF16)
    o_ref[:, 0:LANES] = hi
    o_ref[:, LANES:2 * LANES] = mid
    o_ref[:, 2 * LANES:3 * LANES] = lo


def _forget_cumsum(h, g, sh, sc, wf, bf, S, tm):
    T, D = h.shape
    per_b = S // tm
    return pl.pallas_call(
        functools.partial(_forget_kernel, per_b=per_b),
        grid=(T // tm,),
        in_specs=[
            pl.BlockSpec((tm, D), lambda i: (i, 0)),
            pl.BlockSpec((1, D), lambda i: (0, 0)),
            pl.BlockSpec((None, 1, D), lambda i: (i // per_b, 0, 0)),
            pl.BlockSpec((None, 1, D), lambda i: (i // per_b, 0, 0)),
            pl.BlockSpec((D, LANES), lambda i: (0, 0)),
            pl.BlockSpec((1, LANES), lambda i: (0, 0)),
        ],
        out_specs=pl.BlockSpec((tm, 3 * LANES), lambda i: (i, 0)),
        out_shape=jax.ShapeDtypeStruct((T, 3 * LANES), BF16),
        scratch_shapes=[pltpu.VMEM((1, LANES), F32)],
        compiler_params=_params("arbitrary"),
        name="forget_cumsum",
    )(h, g, sh, sc, wf, bf)


def _fox_prep_kernel(kp_ref, vp_ref, f_ref, pm_ref, ka_ref, vt_ref, kn_ref, *, dh, tk):
    c = pl.program_id(2)
    n_sub = kp_ref.shape[0] // tk
    lane = lax.broadcasted_iota(jnp.int32, (tk, LANES), 1)
    lane8 = lax.broadcasted_iota(jnp.int32, (8, LANES), 1)
    row8 = lax.broadcasted_iota(jnp.int32, (8, LANES), 0)
    row16 = lax.broadcasted_iota(jnp.int32, (DV_ROWS - dh, tk), 0)
    tail = jnp.where(row16 == 0, 1.0, 0.0).astype(BF16)

    @pl.when(c == 0)
    def _():
        kn_ref[...] = jnp.zeros_like(kn_ref)

    for j in range(n_sub):
        rows = slice(j * tk, (j + 1) * tk)
        kp = kp_ref[rows, :]
        ex = jnp.dot(f_ref[rows, :], pm_ref[...], preferred_element_type=F32)
        ex = jnp.where((lane % dh) < 3, 1.0, ex).astype(BF16)
        ka_ref[0, rows, :] = jnp.where(lane < dh, kp, ex)
        ka_ref[1, rows, :] = jnp.where(lane < dh, ex, kp)
        vt = vp_ref[rows, :].astype(F32).T
        vt_ref[0, 0:dh, rows] = vt[0:dh].astype(BF16)
        vt_ref[1, 0:dh, rows] = vt[dh:2 * dh].astype(BF16)
        vt_ref[0, dh:DV_ROWS, rows] = tail
        vt_ref[1, dh:DV_ROWS, rows] = tail
        k2 = kp.astype(F32)
        k2 = k2 * k2
        na = jnp.max(jnp.sum(jnp.where(lane < dh, k2, 0.0), axis=1, keepdims=True), axis=0, keepdims=True)
        nb = jnp.max(jnp.sum(jnp.where(lane < dh, 0.0, k2), axis=1, keepdims=True), axis=0, keepdims=True)
        upd = jnp.where(row8 == 0, na, jnp.where(row8 == 1, nb, 0.0))
        kn_ref[...] = jnp.where(lane8 == c * n_sub + j, upd, kn_ref[...])


def _fox_prep(kv, fparts, pm, B, S, D, dh, tk, tp):
    H = D // dh
    nb = S // tp
    assert S // tk <= LANES and tp % tk == 0
    return pl.pallas_call(
        functools.partial(_fox_prep_kernel, dh=dh, tk=tk),
        grid=(B, H // 2, nb),
        in_specs=[
            pl.BlockSpec((tp, LANES), lambda b, p, c: (b * nb + c, p)),
            pl.BlockSpec((tp, LANES), lambda b, p, c: (b * nb + c, D // LANES + p)),
            pl.BlockSpec((tp, 3 * LANES), lambda b, p, c: (b * nb + c, 0)),
            pl.BlockSpec((None, 3 * LANES, LANES), lambda b, p, c: (p, 0, 0)),
        ],
        out_specs=[
            pl.BlockSpec((None, 2, tp, LANES), lambda b, p, c: (b, p, c, 0)),
            pl.BlockSpec((None, 2, DV_ROWS, tp), lambda b, p, c: (b, p, 0, c)),
            pl.BlockSpec((None, None, 8, LANES), lambda b, p, c: (b, p, 0, 0)),
        ],
        out_shape=[
            jax.ShapeDtypeStruct((B, H, S, LANES), BF16),
            jax.ShapeDtypeStruct((B, H, DV_ROWS, S), BF16),
            jax.ShapeDtypeStruct((B, H // 2, 8, LANES), F32),
        ],
        compiler_params=_params("parallel", "parallel", "arbitrary"),
        name="fox_prep",
    )(kv, kv, fparts, pm)


def _fox_kernel(q_ref, gq_ref, gb_ref, kn_ref, ka_ref, vt_ref, o_ref, sa0, sa1, sb0, sb1, *,
                tq, tk, dh):
    i = pl.program_id(2)
    qT = q_ref[...].T
    g16 = gq_ref[...]
    row8 = lax.broadcasted_iota(jnp.int32, (8, tq), 0)
    xa = jnp.where(row8 < 3, g16[0:8], jnp.where(row8 < 6, 1.0, 0.0))
    xb = jnp.where(row8 < 3, g16[8:16], jnp.where(row8 < 6, 1.0, 0.0))
    zpad = jnp.zeros((LANES - dh - 8, tq), F32)
    qa = jnp.concatenate([qT[0:dh], xa, zpad], axis=0).astype(BF16)
    qb = jnp.concatenate([xb, zpad, qT[dh:2 * dh]], axis=0).astype(BF16)

    def scores(h, q, blk):
        k0 = pl.multiple_of(blk * tk, tk)
        return jnp.dot(ka_ref[h, pl.ds(k0, tk), :], q, preferred_element_type=F32)

    def update(s_ref, h, blk, m, acc):
        k0 = pl.multiple_of(blk * tk, tk)
        s = s_ref[...]
        m_new = jnp.maximum(m, jnp.max(s, axis=0, keepdims=True))
        alpha = jnp.exp2(m - m_new)
        p = jnp.exp2(s - m_new).astype(BF16)
        pv = jnp.dot(vt_ref[h, :, pl.ds(k0, tk)], p, preferred_element_type=F32)
        return m_new, alpha * acc + pv

    q2 = qT * qT
    qna = jnp.max(jnp.sum(q2[0:dh], axis=0, keepdims=True), axis=1, keepdims=True)
    qnb = jnp.max(jnp.sum(q2[dh:2 * dh], axis=0, keepdims=True), axis=1, keepdims=True)
    kn = kn_ref[...]
    gb = gb_ref[...]
    lane = lax.broadcasted_iota(jnp.int32, (1, LANES), 1)
    at_i = lane == i

    def pick(row):
        return jnp.sum(jnp.where(at_i, row, 0.0), axis=1, keepdims=True)

    def needed(qn, knr, glast, gfirst):
        bound = NORM_SLACK * jnp.sqrt(qn) * (jnp.sqrt(knr) + jnp.sqrt(pick(knr))) + pick(gfirst) - glast
        return bound > -SKIP_LOG2

    need = jnp.logical_or(needed(qna, kn[0:1], gb[0:1], gb[2:3]), needed(qnb, kn[1:2], gb[1:2], gb[3:4]))
    lane_f = lane.astype(F32)
    i_f = i.astype(F32)
    cand = jnp.where(jnp.logical_and(need, lane < i), lane_f, i_f)
    jmin = jnp.min(cand).astype(jnp.int32)

    key = lax.broadcasted_iota(jnp.int32, (tk, tq), 0)
    qry = lax.broadcasted_iota(jnp.int32, (tk, tq), 1)
    sa0[...] = jnp.where(key <= qry, scores(0, qa, i), NEG_BIG)
    sb0[...] = jnp.where(key <= qry, scores(1, qb, i), NEG_BIG)
    ma0 = jnp.max(sa0[...], axis=0, keepdims=True)
    mb0 = jnp.max(sb0[...], axis=0, keepdims=True)

    def first(s_ref, h, m0):
        k0 = pl.multiple_of(i * tk, tk)
        p = jnp.exp2(s_ref[...] - m0).astype(BF16)
        return jnp.dot(vt_ref[h, :, pl.ds(k0, tk)], p, preferred_element_type=F32)

    acca0 = first(sa0, 0, ma0)
    accb0 = first(sb0, 1, mb0)

    n_off = i - jmin
    n_pairs = jnp.maximum((n_off + 1) // 2, 1)

    def blk_at(pos):
        return jnp.minimum(jmin + pos, i)

    def masked_scores(h, q, pos):
        return jnp.where(pos < n_off, scores(h, q, blk_at(pos)), NEG_BIG)

    sa0[...] = masked_scores(0, qa, 0)
    sb0[...] = masked_scores(1, qb, 0)

    def pair(u, carry):
        ma, acca, mb, accb = carry
        b0 = jmin + 2 * u
        sa1[...] = scores(0, qa, b0 + 1)
        sb1[...] = scores(1, qb, b0 + 1)
        ma, acca = update(sa0, 0, b0, ma, acca)
        mb, accb = update(sb0, 1, b0, mb, accb)
        sa0[...] = scores(0, qa, b0 + 2)
        sb0[...] = scores(1, qb, b0 + 2)
        ma, acca = update(sa1, 0, b0 + 1, ma, acca)
        mb, accb = update(sb1, 1, b0 + 1, mb, accb)
        return ma, acca, mb, accb

    carry = lax.fori_loop(0, n_pairs - 1, pair, (ma0, acca0, mb0, accb0))
    u = n_pairs - 1

    def tail_two(carry):
        ma, acca, mb, accb = carry
        sa1[...] = scores(0, qa, blk_at(2 * u + 1))
        sb1[...] = scores(1, qb, blk_at(2 * u + 1))
        ma, acca = update(sa0, 0, blk_at(2 * u), ma, acca)
        mb, accb = update(sb0, 1, blk_at(2 * u), mb, accb)
        ma, acca = update(sa1, 0, blk_at(2 * u + 1), ma, acca)
        mb, accb = update(sb1, 1, blk_at(2 * u + 1), mb, accb)
        return ma, acca, mb, accb

    def tail_one(carry):
        ma, acca, mb, accb = carry
        ma, acca = update(sa0, 0, blk_at(2 * u), ma, acca)
        mb, accb = update(sb0, 1, blk_at(2 * u), mb, accb)
        return ma, acca, mb, accb

    ma, acca, mb, accb = lax.cond(2 * u + 1 < n_off, tail_two, tail_one, carry)
    oa = acca[:dh, :] * (1.0 / acca[dh:dh + 1, :])
    ob = accb[:dh, :] * (1.0 / accb[dh:dh + 1, :])
    o_ref[...] = jnp.concatenate([oa, ob], axis=0).T


def _fox_attention(qg, gq, gb, kn, ka, vt, B, S, D, dh, tq):
    T = B * S
    H = D // dh
    nq = S // tq
    return pl.pallas_call(
        functools.partial(_fox_kernel, tq=tq, tk=tq, dh=dh),
        grid=(B, H // 2, nq),
        in_specs=[
            pl.BlockSpec((tq, LANES), lambda b, p, i: (b * nq + i, p)),
            pl.BlockSpec((None, None, 16, tq), lambda b, p, i: (b, p, 0, i)),
            pl.BlockSpec((None, None, 8, LANES), lambda b, p, i: (b, p, 0, 0)),
            pl.BlockSpec((None, None, 8, LANES), lambda b, p, i: (b, p, 0, 0)),
            pl.BlockSpec((None, 2, S, LANES), lambda b, p, i: (b, p, 0, 0)),
            pl.BlockSpec((None, 2, DV_ROWS, S), lambda b, p, i: (b, p, 0, 0)),
        ],
        out_specs=pl.BlockSpec((tq, LANES), lambda b, p, i: (b * nq + i, p)),
        out_shape=jax.ShapeDtypeStruct((T, D), F32),
        scratch_shapes=[pltpu.VMEM((tq, tq), F32) for _ in range(4)],
        compiler_params=_params("parallel", "parallel", "arbitrary"),
        name="fox_attention",
    )(qg, gq, gb, kn, ka, vt)


def _fox_gate_tables(fparts, B, S, H, tk):
    f3 = fparts.reshape(B, S, 3, LANES)[:, :, :, :H].astype(F32)
    gq = f3.reshape(B, S, 3, H // 2, 2).transpose(0, 3, 4, 2, 1)
    gq = jnp.pad(gq, ((0, 0), (0, 0), (0, 0), (0, 5), (0, 0))).reshape(B, H // 2, 16, S)
    G = jnp.sum(f3, axis=2)
    nb = S // tk
    Gb = G.reshape(B, nb, tk, H // 2, 2)
    last = Gb[:, :, tk - 1].transpose(0, 2, 3, 1)
    first = Gb[:, :, 0].transpose(0, 2, 3, 1)
    gb = jnp.concatenate([last, first, jnp.zeros_like(last), jnp.zeros_like(last)], axis=2)
    gb = jnp.pad(gb, ((0, 0), (0, 0), (0, 0), (0, LANES - nb)))
    return gq, gb


def _fox_placement(H, dh):
    p = jnp.arange(H // 2)[:, None, None]
    r = jnp.arange(3 * LANES)[None, :, None]
    l = jnp.arange(LANES)[None, None, :]
    j = r // LANES
    head = r % LANES
    hit_b = jnp.logical_and(head == 2 * p + 1, l == 3 + j)
    hit_a = jnp.logical_and(head == 2 * p, l == dh + 3 + j)
    return jnp.where(jnp.logical_or(hit_a, hit_b), -1.0, 0.0).astype(BF16)


def _final_kernel(h_ref, g_ref, sh_ref, sc_ref, o_ref):
    o_ref[...] = _norm_mod(h_ref[...], g_ref[...], sh_ref[...], sc_ref[...])


def _final_norm(h, g, sh, sc, S, tm):
    T, D = h.shape
    per_b = S // tm
    return pl.pallas_call(
        _final_kernel,
        grid=(T // tm,),
        in_specs=[
            pl.BlockSpec((tm, D), lambda i: (i, 0)),
            pl.BlockSpec((1, D), lambda i: (0, 0)),
            pl.BlockSpec((None, 1, D), lambda i: (i // per_b, 0, 0)),
            pl.BlockSpec((None, 1, D), lambda i: (i // per_b, 0, 0)),
        ],
        out_specs=pl.BlockSpec((tm, D), lambda i: (i, 0)),
        out_shape=jax.ShapeDtypeStruct((T, D), F32),
        compiler_params=_params("parallel"),
        name="final_norm",
    )(h, g, sh, sc)


def _dispatch_tables(top_idx, n_e, tm):
    T, K = top_idx.shape
    n = T * K
    n_tiles = n // tm
    n_units = n_tiles + n_e
    e_flat = top_idx.reshape(n)
    order = jnp.argsort(e_flat, stable=True).astype(jnp.int32)
    pos = jnp.argsort(order).astype(jnp.int32)
    ids = jnp.arange(n_e, dtype=jnp.int32)
    counts = jnp.sum(e_flat[:, None] == ids[None, :], axis=0).astype(jnp.int32)
    ends = jnp.cumsum(counts)
    starts = ends - counts
    t0 = jnp.arange(n_tiles, dtype=jnp.int32)[:, None] * tm
    present = jnp.logical_and(starts[None, :] < t0 + tm, ends[None, :] > t0)
    present = jnp.logical_and(present, counts[None, :] > 0)
    flat = jnp.arange(n_tiles * n_e, dtype=jnp.int32)
    keys = jnp.sort(jnp.where(present.reshape(-1), flat, n_tiles * n_e))[:n_units]
    valid = keys < n_tiles * n_e
    n_valid = jnp.sum(valid.astype(jnp.int32))
    keys = jnp.where(valid, keys, keys[jnp.maximum(n_valid - 1, 0)])
    ut = keys // n_e
    ue = keys % n_e
    ulo = jnp.clip(starts[ue] - ut * tm, 0, tm)
    uhi = jnp.clip(ends[ue] - ut * tm, 0, tm)
    prev_t = jnp.concatenate([jnp.full((1,), -1, jnp.int32), ut[:-1]])
    ufirst = (ut != prev_t).astype(jnp.int32)
    units = tuple(a.astype(jnp.int32) for a in (ut, ue, ulo, uhi, ufirst, valid))
    return order, pos.reshape(T, K), units


def _tile(S, want):
    t = min(S, want)
    assert S % t == 0
    return t


def _mxu_tile(F, cap):
    fits = [t for t in range(MXU_DIM, min(F, cap) + 1, MXU_DIM) if F % t == 0]
    return max(fits) if fits else F


def kernel(x, c, ada_w, ada_b, norm_g, ret_w_in, ret_w_o, kv_ada_w, kv_ada_b, kv_norm_g, fox_w_kv, fox_w_f, fox_b_f, fox_w_qg, fox_w_o, ffn_w_gate, ffn_w_up, ffn_w_down, router_w, router_b, moe_w_gate, moe_w_up, moe_w_down, final_ada_w, final_ada_b, final_norm_g):
    B, S, D = x.shape
    T = B * S
    depth = ada_w.shape[0]
    n_a = ret_w_in.shape[0]
    n_e = router_w.shape[-1]
    dh = D // FOX_HEADS
    dk = D // RET_HEADS
    assert dk == 2 * LANES and S % CHUNK == 0 and n_e <= 8

    tm = _tile(S, 512)
    tm_ffn = _tile(S, 1024)
    tm_moe = _tile(S, 1024)
    tq = _tile(S, 512)
    lc = _tile(S, 512)
    f_dense = ffn_w_gate.shape[-1]
    f_moe = moe_w_gate.shape[-1]
    tf_dense = f_dense // 2 if (f_dense // 2) % LANES == 0 else f_dense
    tf_moe = _mxu_tile(f_moe, 1024)

    c_pad = jnp.zeros((8, D), F32).at[:B].set(c)
    ada = _ada(c_pad, ada_w, ada_b[:, None, :])[:, :B]
    extra_w = jnp.stack([kv_ada_w, final_ada_w])
    extra_b = jnp.stack([kv_ada_b, final_ada_b])[:, None, :]
    extra = _ada(c_pad, extra_w, extra_b)[:, :B]
    vec = lambda a: a[:, None, :]

    h = x.reshape(T, D)
    cos, sin = _rotary_tables(S, dk)
    ret_tabs = _retention_tables()
    shared = None
    w_in_b, w_ro_b = ret_w_in.astype(BF16), ret_w_o.astype(BF16)
    w_kv_b, w_qg_b, w_fo_b = fox_w_kv.astype(BF16)[None], fox_w_qg.astype(BF16), fox_w_o.astype(BF16)
    w_fg_b, w_fu_b, w_fd_b = ffn_w_gate.astype(BF16), ffn_w_up.astype(BF16), ffn_w_down.astype(BF16)

    f_sh, f_sc = [vec(a) for a in jnp.split(extra[1], 2, axis=-1)]
    final = (final_norm_g[None, :], f_sh, f_sc)

    for l in range(depth):
        sh1, sc1, g1, sh2, sc2, g2 = [vec(a) for a in jnp.split(ada[l], 6, axis=-1)]
        ng1 = norm_g[l, 0][None, :]
        ng2 = norm_g[l, 1][None, :]
        if l == n_a:
            kv_sh, kv_sc = [vec(a) for a in jnp.split(extra[0], 2, axis=-1)]
            kvg = kv_norm_g[None, :]
            kv = _norm_mod_matmul(h, kvg, kv_sh, kv_sc, w_kv_b, S,
                                  modes=("plain",) * 4, scales=(1.0,) * 4, cw=D // 2,
                                  out_dtype=BF16, tm=tm)
            wf = jnp.zeros((D, LANES), F32).at[:, :FOX_HEADS].set(fox_w_f).astype(BF16)
            bf = jnp.zeros((1, LANES), F32).at[0, :FOX_HEADS].set(fox_b_f)
            fparts = _forget_cumsum(h, kvg, kv_sh, kv_sc, wf, bf, S, tm)
            ka, vt, kn = _fox_prep(kv, fparts, _fox_placement(FOX_HEADS, dh), B, S, D, dh, tq,
                                   _tile(S, 4 * tq))
            gq, gb = _fox_gate_tables(fparts, B, S, FOX_HEADS, tq)
            shared = (gq, gb, kn, ka, vt)
        if l < n_a:
            qkv = _norm_mod_matmul(
                h, ng1, sh1, sc1, w_in_b, S,
                modes=("rot",) * 4 + ("plain",) * 4,
                scales=(1.0,) * 2 + (dk ** -0.5,) * 2 + (1.0,) * 4,
                cw=D // 2, out_dtype=BF16, tm=tm, rot=(cos, sin), w_layer=l, w_col=0)
            gate = _norm_mod_matmul(h, ng1, sh1, sc1, w_in_b, S,
                                    modes=("plain",) * 4, scales=(1.0,) * 4, cw=D // 2,
                                    out_dtype=F32, tm=tm, w_layer=l, w_col=2)
            y = _retention(qkv, gate, ret_tabs, B, S, lc)
            h = _proj_residual(y, w_ro_b, l, h, g1, S, tm)
        else:
            j = l - n_a
            qg = _norm_mod_matmul(h, ng1, sh1, sc1, w_qg_b, S,
                                  modes=("plain",) * 4, scales=(dh ** -0.5 * LOG2E,) * 2 + (1.0,) * 2,
                                  cw=D // 2, out_dtype=F32, tm=tm, w_layer=j)
            o = _fox_attention(qg, *shared, B, S, D, dh, tq)
            h = _proj_residual(o, w_fo_b, j, h, g1, S, tm, og=qg, og_block=1)
        i = l // 2
        if l % 2 == 0:
            h = _ffn_dense(h, ng2, sh2, sc2, g2, w_fg_b, w_fu_b, w_fd_b, i, S, tm_ffn, tf_dense)
        else:
            rw = jnp.zeros((D, LANES), F32).at[:, :n_e].set(router_w[i]).astype(BF16)
            rb = jnp.zeros((1, LANES), F32).at[0, :n_e].set(router_b[i])
            m, idx8, wt8 = _router(h, ng2, sh2, sc2, rw, rb, S, tm, n_e)
            order, pos, units = _dispatch_tables(idx8[:, :TOP_K], n_e, tm_moe)
            xs = m.at[order // TOP_K].get(mode="promise_in_bounds")
            ys = _ffn_moe(units, xs, moe_w_gate, moe_w_up, moe_w_down, i, tm_moe, tf_moe)
            ya = ys.at[pos[:, 0]].get(mode="promise_in_bounds")
            yb = ys.at[pos[:, 1]].get(mode="promise_in_bounds")
            h = _moe_combine(h, ya, yb, wt8, g2, S, tm, final=final if l == depth - 1 else None)

    if (depth - 1) % 2 == 0:
        h = _final_norm(h, *final, S, tm)
    return h.reshape(B, S, D)
```

```python
import functools
import math

import jax
import jax.numpy as jnp
from jax import lax
from jax.experimental import pallas as pl
from jax.experimental.pallas import tpu as pltpu

F32 = jnp.float32
BF16 = jnp.bfloat16

CHUNK = 64
RET_HEADS = 4
FOX_HEADS = 16
ROPE_BASE = 10000.0
TOP_K = 2
EPS = 1e-6

LANES = 128
MXU_DIM = 256
VMEM_LIMIT_BYTES = 56 * 1024 * 1024
NEG_BIG = -1e30
DV_ROWS = 80
SKIP_LOG2 = 160.0
NORM_SLACK = 1.02
LOG2E = math.log2(math.e)


def _params(*sem):
    return pltpu.CompilerParams(dimension_semantics=sem, vmem_limit_bytes=VMEM_LIMIT_BYTES)


def _silu(x):
    return x * (1.0 / (1.0 + jnp.exp(-x)))


def _sigmoid(x):
    return 1.0 / (1.0 + jnp.exp(-x))


def _norm_mod(x, g, sh, sc):
    var = jnp.mean(x * x, axis=-1, keepdims=True)
    y = (x * lax.rsqrt(var + EPS)) * g
    return y * (1.0 + sc) + sh


def _ada_kernel(c_ref, w_ref, b_ref, o_ref):
    ca = _silu(c_ref[...]).astype(BF16)
    o_ref[...] = jnp.dot(ca, w_ref[...].astype(BF16), preferred_element_type=F32) + b_ref[...]


def _ada(c_pad, w, b, tn=1024):
    L, D, N = w.shape
    return pl.pallas_call(
        _ada_kernel,
        grid=(L, N // tn),
        in_specs=[
            pl.BlockSpec((8, D), lambda l, j: (0, 0)),
            pl.BlockSpec((None, D, tn), lambda l, j: (l, 0, j)),
            pl.BlockSpec((None, 1, tn), lambda l, j: (l, 0, j)),
        ],
        out_specs=pl.BlockSpec((None, 8, tn), lambda l, j: (l, 0, j)),
        out_shape=jax.ShapeDtypeStruct((L, 8, N), F32),
        compiler_params=_params("parallel", "parallel"),
        name="ada_proj",
    )(c_pad, w, b)


def _nmm_kernel(*refs, n_chunks, cw, modes, scales, out_dtype):
    h_ref, g_ref, sh_ref, sc_ref, w_ref = refs[:5]
    has_rot = any(m == "rot" for m in modes)
    if has_rot:
        cos_ref, sin_ref, o_ref = refs[5:8]
    else:
        o_ref = refs[5]
    a = _norm_mod(h_ref[...], g_ref[...], sh_ref[...], sc_ref[...]).astype(BF16)
    for c in range(n_chunks):
        acc = jnp.dot(a, w_ref[:, c * cw:(c + 1) * cw], preferred_element_type=F32)
        if modes[c] == "rot":
            cos = cos_ref[...]
            sin = sin_ref[...]
            parts = []
            for hh in range(cw // (2 * LANES)):
                x1 = acc[:, (2 * hh) * LANES:(2 * hh + 1) * LANES]
                x2 = acc[:, (2 * hh + 1) * LANES:(2 * hh + 2) * LANES]
                parts.append(x1 * cos - x2 * sin)
                parts.append(x2 * cos + x1 * sin)
            acc = jnp.concatenate(parts, axis=-1)
        if scales[c] != 1.0:
            acc = acc * scales[c]
        o_ref[:, c * cw:(c + 1) * cw] = acc.astype(out_dtype)


def _norm_mod_matmul(h, g, sh, sc, w, S, *, modes, scales, cw, out_dtype, tm, rot=None, w_layer=0,
                     w_col=0):
    T, D = h.shape
    n_chunks = len(modes)
    N = n_chunks * cw
    per_b = S // tm
    in_specs = [
        pl.BlockSpec((tm, D), lambda i: (i, 0)),
        pl.BlockSpec((1, D), lambda i: (0, 0)),
        pl.BlockSpec((None, 1, D), lambda i: (i // per_b, 0, 0)),
        pl.BlockSpec((None, 1, D), lambda i: (i // per_b, 0, 0)),
        pl.BlockSpec((None, D, N), lambda i: (w_layer, 0, w_col)),
    ]
    args = [h, g, sh, sc, w]
    if rot is not None:
        in_specs += [pl.BlockSpec((tm, LANES), lambda i: (i % per_b, 0))] * 2
        args += list(rot)
    return pl.pallas_call(
        functools.partial(_nmm_kernel, n_chunks=n_chunks, cw=cw, modes=modes, scales=scales,
                          out_dtype=out_dtype),
        grid=(T // tm,),
        in_specs=in_specs,
        out_specs=pl.BlockSpec((tm, N), lambda i: (i, 0)),
        out_shape=jax.ShapeDtypeStruct((T, N), out_dtype),
        compiler_params=_params("parallel"),
        name="norm_mod_matmul",
    )(*args)


def _ret_kernel(q_ref, k_ref, v_ref, g_ref, intra_ref, qd_ref, kd_ref, cd_ref, o_ref, st_ref, *,
                n_sub, dk, dv):
    @pl.when(pl.program_id(0) == 0)
    def _():
        st_ref[...] = jnp.zeros_like(st_ref)

    def body(c, carry):
        r0 = pl.multiple_of(c * CHUNK, CHUNK)
        rows = pl.ds(r0, CHUNK)
        for b in range(q_ref.shape[0]):
            for h in range(RET_HEADS):
                qc = q_ref[b, rows, h * dk:(h + 1) * dk]
                kc = k_ref[b, rows, h * dk:(h + 1) * dk]
                vc = v_ref[b, rows, h * dv:(h + 1) * dv]
                state = st_ref[b, h]
                scores = lax.dot_general(qc, kc, (((1,), (1,)), ((), ())),
                                         preferred_element_type=F32) * intra_ref[h]
                o = (jnp.dot(scores.astype(BF16), vc, preferred_element_type=F32)
                     + jnp.dot(qc, state.astype(BF16), preferred_element_type=F32) * qd_ref[h])
                kdk = (kc.astype(F32) * kd_ref[h]).astype(BF16)
                st_ref[b, h] = state * cd_ref[h] + lax.dot_general(
                    kdk, vc, (((0,), (0,)), ((), ())), preferred_element_type=F32)
                mu = jnp.mean(o, axis=-1, keepdims=True)
                d = o - mu
                var = jnp.mean(d * d, axis=-1, keepdims=True)
                y = d * lax.rsqrt(var + EPS)
                gate = g_ref[b, rows, h * dv:(h + 1) * dv]
                o_ref[b, rows, h * dv:(h + 1) * dv] = (_silu(gate) * y).astype(o_ref.dtype)
        return carry

    lax.fori_loop(0, n_sub, body, 0)


def _retention(qkv, gate, tabs, B, S, lc):
    T = qkv.shape[0]
    D = qkv.shape[1] // 4
    dk = D // RET_HEADS
    dv = 2 * D // RET_HEADS
    H = RET_HEADS
    intra, qd, kd, cd = tabs
    qkv3 = qkv.reshape(B, S, 4 * D)
    gate3 = gate.reshape(B, S, 2 * D)
    out = pl.pallas_call(
        functools.partial(_ret_kernel, n_sub=lc // CHUNK, dk=dk, dv=dv),
        grid=(S // lc,),
        in_specs=[
            pl.BlockSpec((B, lc, D), lambda s: (0, s, 0)),
            pl.BlockSpec((B, lc, D), lambda s: (0, s, 1)),
            pl.BlockSpec((B, lc, 2 * D), lambda s: (0, s, 1)),
            pl.BlockSpec((B, lc, 2 * D), lambda s: (0, s, 0)),
            pl.BlockSpec((H, CHUNK, CHUNK), lambda s: (0, 0, 0)),
            pl.BlockSpec((H, CHUNK, 1), lambda s: (0, 0, 0)),
            pl.BlockSpec((H, CHUNK, 1), lambda s: (0, 0, 0)),
            pl.BlockSpec((H, 1, 1), lambda s: (0, 0, 0)),
        ],
        out_specs=pl.BlockSpec((B, lc, 2 * D), lambda s: (0, s, 0)),
        out_shape=jax.ShapeDtypeStruct((B, S, 2 * D), BF16),
        scratch_shapes=[pltpu.VMEM((B, H, dk, dv), F32)],
        compiler_params=_params("arbitrary"),
        name="retention",
    )(qkv3, qkv3, qkv3, gate3, intra, qd, kd, cd)
    return out.reshape(T, 2 * D)


def _retention_tables():
    h = jnp.arange(RET_HEADS, dtype=F32)
    log_g = jnp.log(1.0 - 2.0 ** (-5.0 - h))
    idx = jnp.arange(CHUNK, dtype=F32)
    intra = jnp.exp(log_g[:, None, None] * jnp.abs(idx[:, None] - idx[None, :]))
    qd = jnp.exp(log_g[:, None] * (idx[None, :] + 1.0))[:, :, None]
    kd = jnp.exp(log_g[:, None] * (CHUNK - 1.0 - idx[None, :]))[:, :, None]
    cd = jnp.exp(log_g * CHUNK)[:, None, None]
    return intra, qd, kd, cd


def _rotary_tables(S, d):
    inv = 1.0 / (ROPE_BASE ** (jnp.arange(0, d, 2, dtype=F32) / d))
    ang = jnp.arange(S).astype(F32)[:, None] * inv[None, :]
    return jnp.cos(ang), jnp.sin(ang)


def _proj_res_kernel(*refs, gated):
    if gated:
        o_in, og_ref, w_ref, h_ref, gt_ref, out_ref = refs
        x = (o_in[...] * _sigmoid(og_ref[...])).astype(BF16)
    else:
        x_ref, w_ref, h_ref, gt_ref, out_ref = refs
        x = x_ref[...]
    out_ref[...] = h_ref[...] + gt_ref[...] * jnp.dot(x, w_ref[...], preferred_element_type=F32)


def _proj_residual(x, w, w_layer, h, gate, S, tm, og=None, og_block=0):
    T, D = h.shape
    K = w.shape[1]
    per_b = S // tm
    gated = og is not None
    xs = (x, og) if gated else (x,)
    in_specs = [pl.BlockSpec((tm, K), lambda i: (i, 0))]
    if gated:
        in_specs.append(pl.BlockSpec((tm, K), lambda i: (i, og_block)))
    in_specs += [
        pl.BlockSpec((None, K, D), lambda i: (w_layer, 0, 0)),
        pl.BlockSpec((tm, D), lambda i: (i, 0)),
        pl.BlockSpec((None, 1, D), lambda i: (i // per_b, 0, 0)),
    ]
    return pl.pallas_call(
        functools.partial(_proj_res_kernel, gated=gated),
        grid=(T // tm,),
        in_specs=in_specs,
        out_specs=pl.BlockSpec((tm, D), lambda i: (i, 0)),
        out_shape=jax.ShapeDtypeStruct((T, D), F32),
        compiler_params=_params("parallel"),
        name="proj_residual",
    )(*xs, w, h, gate)


def _ffn_dense_kernel(h_ref, g_ref, sh_ref, sc_ref, gt_ref, wg_ref, wu_ref, wd_ref, o_ref, a_scr,
                      acc_scr):
    f = pl.program_id(1)

    @pl.when(f == 0)
    def _():
        a_scr[...] = _norm_mod(h_ref[...], g_ref[...], sh_ref[...], sc_ref[...]).astype(BF16)
        acc_scr[...] = jnp.zeros_like(acc_scr)

    a = a_scr[...]
    gate = jnp.dot(a, wg_ref[...], preferred_element_type=F32)
    up = jnp.dot(a, wu_ref[...], preferred_element_type=F32)
    act = (_silu(gate) * up).astype(BF16)
    acc_scr[...] += jnp.dot(act, wd_ref[...], preferred_element_type=F32)

    @pl.when(f == pl.num_programs(1) - 1)
    def _():
        o_ref[...] = h_ref[...] + gt_ref[...] * acc_scr[...]


def _ffn_dense(h, g, sh, sc, gate, wg, wu, wd, layer, S, tm, tf):
    T, D = h.shape
    F = wg.shape[2]
    per_b = S // tm
    return pl.pallas_call(
        _ffn_dense_kernel,
        grid=(T // tm, F // tf),
        in_specs=[
            pl.BlockSpec((tm, D), lambda i, f: (i, 0)),
            pl.BlockSpec((1, D), lambda i, f: (0, 0)),
            pl.BlockSpec((None, 1, D), lambda i, f: (i // per_b, 0, 0)),
            pl.BlockSpec((None, 1, D), lambda i, f: (i // per_b, 0, 0)),
            pl.BlockSpec((None, 1, D), lambda i, f: (i // per_b, 0, 0)),
            pl.BlockSpec((None, D, tf), lambda i, f: (layer, 0, f)),
            pl.BlockSpec((None, D, tf), lambda i, f: (layer, 0, f)),
            pl.BlockSpec((None, tf, D), lambda i, f: (layer, f, 0)),
        ],
        out_specs=pl.BlockSpec((tm, D), lambda i, f: (i, 0)),
        out_shape=jax.ShapeDtypeStruct((T, D), F32),
        scratch_shapes=[pltpu.VMEM((tm, D), BF16), pltpu.VMEM((tm, D), F32)],
        compiler_params=_params("parallel", "arbitrary"),
        name="ffn_dense",
    )(h, g, sh, sc, gate, wg, wu, wd)


def _ffn_moe_kernel(ut_ref, ue_ref, ulo_ref, uhi_ref, ufirst_ref, uvalid_ref, x_ref, wg_ref, wu_ref,
                    wd_ref, o_ref, acc_scr):
    u = pl.program_id(0)
    f = pl.program_id(1)
    valid = uvalid_ref[u] > 0

    @pl.when(jnp.logical_and(jnp.logical_and(valid, ufirst_ref[u] > 0), f == 0))
    def _():
        acc_scr[...] = jnp.zeros_like(acc_scr)

    tm = x_ref.shape[0]
    half = tm // 2
    lo = ulo_ref[u]
    hi = uhi_ref[u]
    whole = jnp.logical_and(lo == 0, hi == tm)

    def swiglu(rows, r0, masked):
        x = x_ref[rows, :]
        gate = jnp.dot(x, wg_ref[...].astype(BF16), preferred_element_type=F32)
        up = jnp.dot(x, wu_ref[...].astype(BF16), preferred_element_type=F32)
        act = _silu(gate) * up
        if masked:
            row = r0 + lax.broadcasted_iota(jnp.int32, gate.shape, 0)
            act = jnp.where(jnp.logical_and(row >= lo, row < hi), act, 0.0)
        acc_scr[rows, :] += jnp.dot(act.astype(BF16), wd_ref[...].astype(BF16),
                                    preferred_element_type=F32)

    @pl.when(jnp.logical_and(valid, whole))
    def _():
        swiglu(slice(None), 0, False)

    for k in range(2):
        touches = jnp.logical_and(lo < (k + 1) * half, hi > k * half)

        @pl.when(jnp.logical_and(jnp.logical_and(valid, jnp.logical_not(whole)), touches))
        def _(k=k):
            swiglu(slice(k * half, (k + 1) * half), k * half, True)

    @pl.when(jnp.logical_and(valid, f == pl.num_programs(1) - 1))
    def _():
        o_ref[...] = acc_scr[...]


def _ffn_moe(units, xs, wg, wu, wd, layer, tm, tf):
    N, D = xs.shape
    F = wg.shape[3]
    nf = F // tf
    n_units = units[0].shape[0]

    def xrow(u, f, ut, ue, ulo, uhi, ufirst, uvalid):
        return (ut[u], 0)

    def wcol(u, f, ut, ue, ulo, uhi, ufirst, uvalid):
        return (layer, ue[u], 0, jnp.where(uvalid[u] > 0, f, nf - 1))

    def wrow(u, f, ut, ue, ulo, uhi, ufirst, uvalid):
        return (layer, ue[u], jnp.where(uvalid[u] > 0, f, nf - 1), 0)

    grid_spec = pltpu.PrefetchScalarGridSpec(
        num_scalar_prefetch=6,
        grid=(n_units, nf),
        in_specs=[
            pl.BlockSpec((tm, D), xrow),
            pl.BlockSpec((None, None, D, tf), wcol),
            pl.BlockSpec((None, None, D, tf), wcol),
            pl.BlockSpec((None, None, tf, D), wrow),
        ],
        out_specs=pl.BlockSpec((tm, D), xrow),
        scratch_shapes=[pltpu.VMEM((tm, D), F32)],
    )
    return pl.pallas_call(
        _ffn_moe_kernel,
        grid_spec=grid_spec,
        out_shape=jax.ShapeDtypeStruct((N, D), F32),
        compiler_params=_params("arbitrary", "arbitrary"),
        name="ffn_moe",
    )(*units, xs, wg, wu, wd)


def _combine_kernel(*refs, fused_final):
    h_ref, ya_ref, yb_ref, wt_ref, gt_ref = refs[:5]
    o_ref = refs[-1]
    wt = wt_ref[...]
    ff = wt[:, 0:1] * ya_ref[...] + wt[:, 1:2] * yb_ref[...]
    out = h_ref[...] + gt_ref[...] * ff
    if fused_final:
        g_ref, sh_ref, sc_ref = refs[5:8]
        out = _norm_mod(out, g_ref[...], sh_ref[...], sc_ref[...])
    o_ref[...] = out


def _moe_combine(h, ya, yb, wt8, gate, S, tm, final=None):
    T, D = h.shape
    per_b = S // tm
    row = pl.BlockSpec((tm, D), lambda i: (i, 0))
    per_batch = pl.BlockSpec((None, 1, D), lambda i: (i // per_b, 0, 0))
    in_specs = [row, row, row, pl.BlockSpec((tm, 8), lambda i: (i, 0)), per_batch]
    args = [h, ya, yb, wt8, gate]
    if final is not None:
        in_specs += [pl.BlockSpec((1, D), lambda i: (0, 0)), per_batch, per_batch]
        args += list(final)
    return pl.pallas_call(
        functools.partial(_combine_kernel, fused_final=final is not None),
        grid=(T // tm,),
        in_specs=in_specs,
        out_specs=row,
        out_shape=jax.ShapeDtypeStruct((T, D), F32),
        compiler_params=_params("parallel"),
        name="moe_combine",
    )(*args)


def _router_kernel(h_ref, g_ref, sh_ref, sc_ref, rw_ref, rb_ref, m_ref, idx_ref, wt_ref, *, n_e):
    a = _norm_mod(h_ref[...], g_ref[...], sh_ref[...], sc_ref[...]).astype(BF16)
    m_ref[...] = a
    logits = jnp.dot(a, rw_ref[...], preferred_element_type=F32) + rb_ref[...]
    lane = lax.broadcasted_iota(jnp.int32, logits.shape, 1)
    lane_f = lane.astype(F32)
    logits = jnp.where(lane < n_e, logits, NEG_BIG)
    m1 = jnp.max(logits, axis=-1, keepdims=True)
    i1 = jnp.min(jnp.where(logits == m1, lane_f, float(LANES)), axis=-1, keepdims=True)
    rest = jnp.where(lane_f == i1, NEG_BIG, logits)
    m2 = jnp.max(rest, axis=-1, keepdims=True)
    i2 = jnp.min(jnp.where(rest == m2, lane_f, float(LANES)), axis=-1, keepdims=True)
    e2 = jnp.exp(m2 - m1)
    w1 = 1.0 / (1.0 + e2)
    w2 = e2 * w1
    w = jnp.where(lane == 0, w1, jnp.where(lane == 1, w2, 0.0))
    ii = jnp.where(lane == 0, i1, jnp.where(lane == 1, i2, 0.0)).astype(jnp.int32)
    wt_ref[...] = w[:, :8]
    idx_ref[...] = ii[:, :8]


def _router(h, g, sh, sc, rw, rb, S, tm, n_e):
    T, D = h.shape
    per_b = S // tm
    return pl.pallas_call(
        functools.partial(_router_kernel, n_e=n_e),
        grid=(T // tm,),
        in_specs=[
            pl.BlockSpec((tm, D), lambda i: (i, 0)),
            pl.BlockSpec((1, D), lambda i: (0, 0)),
            pl.BlockSpec((None, 1, D), lambda i: (i // per_b, 0, 0)),
            pl.BlockSpec((None, 1, D), lambda i: (i // per_b, 0, 0)),
            pl.BlockSpec((D, LANES), lambda i: (0, 0)),
            pl.BlockSpec((1, LANES), lambda i: (0, 0)),
        ],
        out_specs=[
            pl.BlockSpec((tm, D), lambda i: (i, 0)),
            pl.BlockSpec((tm, 8), lambda i: (i, 0)),
            pl.BlockSpec((tm, 8), lambda i: (i, 0)),
        ],
        out_shape=[
            jax.ShapeDtypeStruct((T, D), BF16),
            jax.ShapeDtypeStruct((T, 8), jnp.int32),
            jax.ShapeDtypeStruct((T, 8), F32),
        ],
        compiler_params=_params("parallel"),
        name="router",
    )(h, g, sh, sc, rw, rb)


def _forget_kernel(h_ref, g_ref, sh_ref, sc_ref, wf_ref, bf_ref, o_ref, carry_ref, *, per_b):
    @pl.when(pl.program_id(0) % per_b == 0)
    def _():
        carry_ref[...] = jnp.zeros_like(carry_ref)

    a = _norm_mod(h_ref[...], g_ref[...], sh_ref[...], sc_ref[...]).astype(BF16)
    z = jnp.dot(a, wf_ref[...], preferred_element_type=F32) + bf_ref[...]
    lf = jnp.minimum(z, 0.0) - jnp.log(1.0 + jnp.exp(-jnp.abs(z)))
    tm = lf.shape[0]
    r = lax.broadcasted_iota(jnp.int32, (tm, tm), 0)
    c = lax.broadcasted_iota(jnp.int32, (tm, tm), 1)
    tri = jnp.where(c <= r, 1.0, 0.0).astype(F32)
    cs = jnp.dot(tri, lf, preferred_element_type=F32, precision=lax.Precision.HIGHEST)
    out = cs + carry_ref[...]
    carry_ref[...] = out[tm - 1:tm, :]
    out2 = out * LOG2E
    hi = out2.astype(BF16)
    r1 = out2 - hi.astype(F32)
    mid = r1.astype(BF16)
    lo = (r1 - mid.astype(F32)).astype(BF16)
    o_ref[:, 0:LANES] = hi
    o_ref[:, LANES:2 * LANES] = mid
    o_ref[:, 2 * LANES:3 * LANES] = lo


def _forget_cumsum(h, g, sh, sc, wf, bf, S, tm):
    T, D = h.shape
    per_b = S // tm
    return pl.pallas_call(
        functools.partial(_forget_kernel, per_b=per_b),
        grid=(T // tm,),
        in_specs=[
            pl.BlockSpec((tm, D), lambda i: (i, 0)),
            pl.BlockSpec((1, D), lambda i: (0, 0)),
            pl.BlockSpec((None, 1, D), lambda i: (i // per_b, 0, 0)),
            pl.BlockSpec((None, 1, D), lambda i: (i // per_b, 0, 0)),
            pl.BlockSpec((D, LANES), lambda i: (0, 0)),
            pl.BlockSpec((1, LANES), lambda i: (0, 0)),
        ],
        out_specs=pl.BlockSpec((tm, 3 * LANES), lambda i: (i, 0)),
        out_shape=jax.ShapeDtypeStruct((T, 3 * LANES), BF16),
        scratch_shapes=[pltpu.VMEM((1, LANES), F32)],
        compiler_params=_params("arbitrary"),
        name="forget_cumsum",
    )(h, g, sh, sc, wf, bf)


def _fox_prep_kernel(kp_ref, vp_ref, f_ref, pm_ref, ka_ref, vt_ref, kn_ref, *, dh, tk):
    c = pl.program_id(2)
    n_sub = kp_ref.shape[0] // tk
    lane = lax.broadcasted_iota(jnp.int32, (tk, LANES), 1)
    lane8 = lax.broadcasted_iota(jnp.int32, (8, LANES), 1)
    row8 = lax.broadcasted_iota(jnp.int32, (8, LANES), 0)
    row16 = lax.broadcasted_iota(jnp.int32, (DV_ROWS - dh, tk), 0)
    tail = jnp.where(row16 == 0, 1.0, 0.0).astype(BF16)

    @pl.when(c == 0)
    def _():
        kn_ref[...] = jnp.zeros_like(kn_ref)

    for j in range(n_sub):
        rows = slice(j * tk, (j + 1) * tk)
        kp = kp_ref[rows, :]
        ex = jnp.dot(f_ref[rows, :], pm_ref[...], preferred_element_type=F32)
        ex = jnp.where((lane % dh) < 3, 1.0, ex).astype(BF16)
        ka_ref[0, rows, :] = jnp.where(lane < dh, kp, ex)
        ka_ref[1, rows, :] = jnp.where(lane < dh, ex, kp)
        vt = vp_ref[rows, :].astype(F32).T
        vt_ref[0, 0:dh, rows] = vt[0:dh].astype(BF16)
        vt_ref[1, 0:dh, rows] = vt[dh:2 * dh].astype(BF16)
        vt_ref[0, dh:DV_ROWS, rows] = tail
        vt_ref[1, dh:DV_ROWS, rows] = tail
        k2 = kp.astype(F32)
        k2 = k2 * k2
        na = jnp.max(jnp.sum(jnp.where(lane < dh, k2, 0.0), axis=1, keepdims=True), axis=0, keepdims=True)
        nb = jnp.max(jnp.sum(jnp.where(lane < dh, 0.0, k2), axis=1, keepdims=True), axis=0, keepdims=True)
        upd = jnp.where(row8 == 0, na, jnp.where(row8 == 1, nb, 0.0))
        kn_ref[...] = jnp.where(lane8 == c * n_sub + j, upd, kn_ref[...])


def _fox_prep(kv, fparts, pm, B, S, D, dh, tk, tp):
    H = D // dh
    nb = S // tp
    assert S // tk <= LANES and tp % tk == 0
    return pl.pallas_call(
        functools.partial(_fox_prep_kernel, dh=dh, tk=tk),
        grid=(B, H // 2, nb),
        in_specs=[
            pl.BlockSpec((tp, LANES), lambda b, p, c: (b * nb + c, p)),
            pl.BlockSpec((tp, LANES), lambda b, p, c: (b * nb + c, D // LANES + p)),
            pl.BlockSpec((tp, 3 * LANES), lambda b, p, c: (b * nb + c, 0)),
            pl.BlockSpec((None, 3 * LANES, LANES), lambda b, p, c: (p, 0, 0)),
        ],
        out_specs=[
            pl.BlockSpec((None, 2, tp, LANES), lambda b, p, c: (b, p, c, 0)),
            pl.BlockSpec((None, 2, DV_ROWS, tp), lambda b, p, c: (b, p, 0, c)),
            pl.BlockSpec((None, None, 8, LANES), lambda b, p, c: (b, p, 0, 0)),
        ],
        out_shape=[
            jax.ShapeDtypeStruct((B, H, S, LANES), BF16),
            jax.ShapeDtypeStruct((B, H, DV_ROWS, S), BF16),
            jax.ShapeDtypeStruct((B, H // 2, 8, LANES), F32),
        ],
        compiler_params=_params("parallel", "parallel", "arbitrary"),
        name="fox_prep",
    )(kv, kv, fparts, pm)


def _fox_kernel(q_ref, gq_ref, gb_ref, kn_ref, ka_ref, vt_ref, o_ref, sa0, sa1, sb0, sb1, *,
                tq, tk, dh):
    i = pl.program_id(2)
    qT = q_ref[...].T
    g16 = gq_ref[...]
    row8 = lax.broadcasted_iota(jnp.int32, (8, tq), 0)
    xa = jnp.where(row8 < 3, g16[0:8], jnp.where(row8 < 6, 1.0, 0.0))
    xb = jnp.where(row8 < 3, g16[8:16], jnp.where(row8 < 6, 1.0, 0.0))
    zpad = jnp.zeros((LANES - dh - 8, tq), F32)
    qa = jnp.concatenate([qT[0:dh], xa, zpad], axis=0).astype(BF16)
    qb = jnp.concatenate([xb, zpad, qT[dh:2 * dh]], axis=0).astype(BF16)

    def scores(h, q, blk):
        k0 = pl.multiple_of(blk * tk, tk)
        return jnp.dot(ka_ref[h, pl.ds(k0, tk), :], q, preferred_element_type=F32)

    def update(s_ref, h, blk, m, acc):
        k0 = pl.multiple_of(blk * tk, tk)
        s = s_ref[...]
        m_new = jnp.maximum(m, jnp.max(s, axis=0, keepdims=True))
        alpha = jnp.exp2(m - m_new)
        p = jnp.exp2(s - m_new).astype(BF16)
        pv = jnp.dot(vt_ref[h, :, pl.ds(k0, tk)], p, preferred_element_type=F32)
        return m_new, alpha * acc + pv

    q2 = qT * qT
    qna = jnp.max(jnp.sum(q2[0:dh], axis=0, keepdims=True), axis=1, keepdims=True)
    qnb = jnp.max(jnp.sum(q2[dh:2 * dh], axis=0, keepdims=True), axis=1, keepdims=True)
    kn = kn_ref[...]
    gb = gb_ref[...]
    lane = lax.broadcasted_iota(jnp.int32, (1, LANES), 1)
    at_i = lane == i

    def pick(row):
        return jnp.sum(jnp.where(at_i, row, 0.0), axis=1, keepdims=True)

    def needed(qn, knr, glast, gfirst):
        bound = NORM_SLACK * jnp.sqrt(qn) * (jnp.sqrt(knr) + jnp.sqrt(pick(knr))) + pick(gfirst) - glast
        return bound > -SKIP_LOG2

    need = jnp.logical_or(needed(qna, kn[0:1], gb[0:1], gb[2:3]), needed(qnb, kn[1:2], gb[1:2], gb[3:4]))
    lane_f = lane.astype(F32)
    i_f = i.astype(F32)
    cand = jnp.where(jnp.logical_and(need, lane < i), lane_f, i_f)
    jmin = jnp.min(cand).astype(jnp.int32)

    key = lax.broadcasted_iota(jnp.int32, (tk, tq), 0)
    qry = lax.broadcasted_iota(jnp.int32, (tk, tq), 1)
    sa0[...] = jnp.where(key <= qry, scores(0, qa, i), NEG_BIG)
    sb0[...] = jnp.where(key <= qry, scores(1, qb, i), NEG_BIG)
    ma0 = jnp.max(sa0[...], axis=0, keepdims=True)
    mb0 = jnp.max(sb0[...], axis=0, keepdims=True)

    prev = jnp.maximum(i - 1, 0)
    sa1[...] = jnp.where(i > 0, scores(0, qa, prev), NEG_BIG)
    sb1[...] = jnp.where(i > 0, scores(1, qb, prev), NEG_BIG)

    def first(s_ref, h, m0):
        k0 = pl.multiple_of(i * tk, tk)
        p = jnp.exp2(s_ref[...] - m0).astype(BF16)
        return jnp.dot(vt_ref[h, :, pl.ds(k0, tk)], p, preferred_element_type=F32)

    acca0 = first(sa0, 0, ma0)
    accb0 = first(sb0, 1, mb0)
    ma1, acca1 = update(sa1, 0, prev, ma0, acca0)
    mb1, accb1 = update(sb1, 1, prev, mb0, accb0)

    n_off = jnp.maximum(i - 1 - jmin, 0)
    n_pairs = jnp.maximum((n_off + 1) // 2, 1)

    def blk_at(pos):
        return jnp.minimum(jmin + pos, i)

    def masked_scores(h, q, pos):
        return jnp.where(pos < n_off, scores(h, q, blk_at(pos)), NEG_BIG)

    sa0[...] = masked_scores(0, qa, 0)
    sb0[...] = masked_scores(1, qb, 0)

    def pair(u, carry):
        ma, acca, mb, accb = carry
        b0 = jmin + 2 * u
        sa1[...] = scores(0, qa, b0 + 1)
        sb1[...] = scores(1, qb, b0 + 1)
        ma, acca = update(sa0, 0, b0, ma, acca)
        mb, accb = update(sb0, 1, b0, mb, accb)
        sa0[...] = scores(0, qa, b0 + 2)
        sb0[...] = scores(1, qb, b0 + 2)
        ma, acca = update(sa1, 0, b0 + 1, ma, acca)
        mb, accb = update(sb1, 1, b0 + 1, mb, accb)
        return ma, acca, mb, accb

    carry = lax.fori_loop(0, n_pairs - 1, pair, (ma1, acca1, mb1, accb1))
    u = n_pairs - 1

    def tail_two(carry):
        ma, acca, mb, accb = carry
        sa1[...] = scores(0, qa, blk_at(2 * u + 1))
        sb1[...] = scores(1, qb, blk_at(2 * u + 1))
        ma, acca = update(sa0, 0, blk_at(2 * u), ma, acca)
        mb, accb = update(sb0, 1, blk_at(2 * u), mb, accb)
        ma, acca = update(sa1, 0, blk_at(2 * u + 1), ma, acca)
        mb, accb = update(sb1, 1, blk_at(2 * u + 1), mb, accb)
        return ma, acca, mb, accb

    def tail_one(carry):
        ma, acca, mb, accb = carry
        ma, acca = update(sa0, 0, blk_at(2 * u), ma, acca)
        mb, accb = update(sb0, 1, blk_at(2 * u), mb, accb)
        return ma, acca, mb, accb

    ma, acca, mb, accb = lax.cond(2 * u + 1 < n_off, tail_two, tail_one, carry)
    oa = acca[:dh, :] * (1.0 / acca[dh:dh + 1, :])
    ob = accb[:dh, :] * (1.0 / accb[dh:dh + 1, :])
    o_ref[...] = jnp.concatenate([oa, ob], axis=0).T


def _fox_attention(qg, gq, gb, kn, ka, vt, B, S, D, dh, tq):
    T = B * S
    H = D // dh
    nq = S // tq
    return pl.pallas_call(
        functools.partial(_fox_kernel, tq=tq, tk=tq, dh=dh),
        grid=(B, H // 2, nq),
        in_specs=[
            pl.BlockSpec((tq, LANES), lambda b, p, i: (b * nq + i, p)),
            pl.BlockSpec((None, None, 16, tq), lambda b, p, i: (b, p, 0, i)),
            pl.BlockSpec((None, None, 8, LANES), lambda b, p, i: (b, p, 0, 0)),
            pl.BlockSpec((None, None, 8, LANES), lambda b, p, i: (b, p, 0, 0)),
            pl.BlockSpec((None, 2, S, LANES), lambda b, p, i: (b, p, 0, 0)),
            pl.BlockSpec((None, 2, DV_ROWS, S), lambda b, p, i: (b, p, 0, 0)),
        ],
        out_specs=pl.BlockSpec((tq, LANES), lambda b, p, i: (b * nq + i, p)),
        out_shape=jax.ShapeDtypeStruct((T, D), F32),
        scratch_shapes=[pltpu.VMEM((tq, tq), F32) for _ in range(4)],
        compiler_params=_params("parallel", "parallel", "arbitrary"),
        name="fox_attention",
    )(qg, gq, gb, kn, ka, vt)


def _fox_gate_tables(fparts, B, S, H, tk):
    f3 = fparts.reshape(B, S, 3, LANES)[:, :, :, :H].astype(F32)
    gq = f3.reshape(B, S, 3, H // 2, 2).transpose(0, 3, 4, 2, 1)
    gq = jnp.pad(gq, ((0, 0), (0, 0), (0, 0), (0, 5), (0, 0))).reshape(B, H // 2, 16, S)
    G = jnp.sum(f3, axis=2)
    nb = S // tk
    Gb = G.reshape(B, nb, tk, H // 2, 2)
    last = Gb[:, :, tk - 1].transpose(0, 2, 3, 1)
    first = Gb[:, :, 0].transpose(0, 2, 3, 1)
    gb = jnp.concatenate([last, first, jnp.zeros_like(last), jnp.zeros_like(last)], axis=2)
    gb = jnp.pad(gb, ((0, 0), (0, 0), (0, 0), (0, LANES - nb)))
    return gq, gb


def _fox_placement(H, dh):
    p = jnp.arange(H // 2)[:, None, None]
    r = jnp.arange(3 * LANES)[None, :, None]
    l = jnp.arange(LANES)[None, None, :]
    j = r // LANES
    head = r % LANES
    hit_b = jnp.logical_and(head == 2 * p + 1, l == 3 + j)
    hit_a = jnp.logical_and(head == 2 * p, l == dh + 3 + j)
    return jnp.where(jnp.logical_or(hit_a, hit_b), -1.0, 0.0).astype(BF16)


def _final_kernel(h_ref, g_ref, sh_ref, sc_ref, o_ref):
    o_ref[...] = _norm_mod(h_ref[...], g_ref[...], sh_ref[...], sc_ref[...])


def _final_norm(h, g, sh, sc, S, tm):
    T, D = h.shape
    per_b = S // tm
    return pl.pallas_call(
        _final_kernel,
        grid=(T // tm,),
        in_specs=[
            pl.BlockSpec((tm, D), lambda i: (i, 0)),
            pl.BlockSpec((1, D), lambda i: (0, 0)),
            pl.BlockSpec((None, 1, D), lambda i: (i // per_b, 0, 0)),
            pl.BlockSpec((None, 1, D), lambda i: (i // per_b, 0, 0)),
        ],
        out_specs=pl.BlockSpec((tm, D), lambda i: (i, 0)),
        out_shape=jax.ShapeDtypeStruct((T, D), F32),
        compiler_params=_params("parallel"),
        name="final_norm",
    )(h, g, sh, sc)


def _dispatch_tables(top_idx, n_e, tm):
    T, K = top_idx.shape
    n = T * K
    n_tiles = n // tm
    n_units = n_tiles + n_e
    e_flat = top_idx.reshape(n)
    order = jnp.argsort(e_flat, stable=True).astype(jnp.int32)
    pos = jnp.argsort(order).astype(jnp.int32)
    ids = jnp.arange(n_e, dtype=jnp.int32)
    counts = jnp.sum(e_flat[:, None] == ids[None, :], axis=0).astype(jnp.int32)
    ends = jnp.cumsum(counts)
    starts = ends - counts
    t0 = jnp.arange(n_tiles, dtype=jnp.int32)[:, None] * tm
    present = jnp.logical_and(starts[None, :] < t0 + tm, ends[None, :] > t0)
    present = jnp.logical_and(present, counts[None, :] > 0)
    flat = jnp.arange(n_tiles * n_e, dtype=jnp.int32)
    keys = jnp.sort(jnp.where(present.reshape(-1), flat, n_tiles * n_e))[:n_units]
    valid = keys < n_tiles * n_e
    n_valid = jnp.sum(valid.astype(jnp.int32))
    keys = jnp.where(valid, keys, keys[jnp.maximum(n_valid - 1, 0)])
    ut = keys // n_e
    ue = keys % n_e
    ulo = jnp.clip(starts[ue] - ut * tm, 0, tm)
    uhi = jnp.clip(ends[ue] - ut * tm, 0, tm)
    prev_t = jnp.concatenate([jnp.full((1,), -1, jnp.int32), ut[:-1]])
    ufirst = (ut != prev_t).astype(jnp.int32)
    units = tuple(a.astype(jnp.int32) for a in (ut, ue, ulo, uhi, ufirst, valid))
    return order, pos.reshape(T, K), units


def _tile(S, want):
    t = min(S, want)
    assert S % t == 0
    return t


def _mxu_tile(F, cap):
    fits = [t for t in range(MXU_DIM, min(F, cap) + 1, MXU_DIM) if F % t == 0]
    return max(fits) if fits else F


def kernel(x, c, ada_w, ada_b, norm_g, ret_w_in, ret_w_o, kv_ada_w, kv_ada_b, kv_norm_g, fox_w_kv, fox_w_f, fox_b_f, fox_w_qg, fox_w_o, ffn_w_gate, ffn_w_up, ffn_w_down, router_w, router_b, moe_w_gate, moe_w_up, moe_w_down, final_ada_w, final_ada_b, final_norm_g):
    B, S, D = x.shape
    T = B * S
    depth = ada_w.shape[0]
    n_a = ret_w_in.shape[0]
    n_e = router_w.shape[-1]
    dh = D // FOX_HEADS
    dk = D // RET_HEADS
    assert dk == 2 * LANES and S % CHUNK == 0 and n_e <= 8

    tm = _tile(S, 512)
    tm_ffn = _tile(S, 1024)
    tm_moe = _tile(S, 1024)
    tq = _tile(S, 512)
    lc = _tile(S, 256)
    f_dense = ffn_w_gate.shape[-1]
    f_moe = moe_w_gate.shape[-1]
    tf_dense = f_dense // 2 if (f_dense // 2) % LANES == 0 else f_dense
    tf_moe = _mxu_tile(f_moe, 1024)

    c_pad = jnp.zeros((8, D), F32).at[:B].set(c)
    ada = _ada(c_pad, ada_w, ada_b[:, None, :])[:, :B]
    extra_w = jnp.stack([kv_ada_w, final_ada_w])
    extra_b = jnp.stack([kv_ada_b, final_ada_b])[:, None, :]
    extra = _ada(c_pad, extra_w, extra_b)[:, :B]
    vec = lambda a: a[:, None, :]

    h = x.reshape(T, D)
    cos, sin = _rotary_tables(S, dk)
    ret_tabs = _retention_tables()
    shared = None
    w_in_b, w_ro_b = ret_w_in.astype(BF16), ret_w_o.astype(BF16)
    w_kv_b, w_qg_b, w_fo_b = fox_w_kv.astype(BF16)[None], fox_w_qg.astype(BF16), fox_w_o.astype(BF16)
    w_fg_b, w_fu_b, w_fd_b = ffn_w_gate.astype(BF16), ffn_w_up.astype(BF16), ffn_w_down.astype(BF16)

    f_sh, f_sc = [vec(a) for a in jnp.split(extra[1], 2, axis=-1)]
    final = (final_norm_g[None, :], f_sh, f_sc)

    for l in range(depth):
        sh1, sc1, g1, sh2, sc2, g2 = [vec(a) for a in jnp.split(ada[l], 6, axis=-1)]
        ng1 = norm_g[l, 0][None, :]
        ng2 = norm_g[l, 1][None, :]
        if l == n_a:
            kv_sh, kv_sc = [vec(a) for a in jnp.split(extra[0], 2, axis=-1)]
            kvg = kv_norm_g[None, :]
            kv = _norm_mod_matmul(h, kvg, kv_sh, kv_sc, w_kv_b, S,
                                  modes=("plain",) * 4, scales=(1.0,) * 4, cw=D // 2,
                                  out_dtype=BF16, tm=tm)
            wf = jnp.zeros((D, LANES), F32).at[:, :FOX_HEADS].set(fox_w_f).astype(BF16)
            bf = jnp.zeros((1, LANES), F32).at[0, :FOX_HEADS].set(fox_b_f)
            fparts = _forget_cumsum(h, kvg, kv_sh, kv_sc, wf, bf, S, tm)
            ka, vt, kn = _fox_prep(kv, fparts, _fox_placement(FOX_HEADS, dh), B, S, D, dh, tq,
                                   _tile(S, 4 * tq))
            gq, gb = _fox_gate_tables(fparts, B, S, FOX_HEADS, tq)
            shared = (gq, gb, kn, ka, vt)
        if l < n_a:
            qkv = _norm_mod_matmul(
                h, ng1, sh1, sc1, w_in_b, S,
                modes=("rot",) * 4 + ("plain",) * 4,
                scales=(1.0,) * 2 + (dk ** -0.5,) * 2 + (1.0,) * 4,
                cw=D // 2, out_dtype=BF16, tm=tm, rot=(cos, sin), w_layer=l, w_col=0)
            gate = _norm_mod_matmul(h, ng1, sh1, sc1, w_in_b, S,
                                    modes=("plain",) * 4, scales=(1.0,) * 4, cw=D // 2,
                                    out_dtype=F32, tm=tm, w_layer=l, w_col=2)
            y = _retention(qkv, gate, ret_tabs, B, S, lc)
            h = _proj_residual(y, w_ro_b, l, h, g1, S, tm)
        else:
            j = l - n_a
            qg = _norm_mod_matmul(h, ng1, sh1, sc1, w_qg_b, S,
                                  modes=("plain",) * 4, scales=(dh ** -0.5 * LOG2E,) * 2 + (1.0,) * 2,
                                  cw=D // 2, out_dtype=F32, tm=tm, w_layer=j)
            o = _fox_attention(qg, *shared, B, S, D, dh, tq)
            h = _proj_residual(o, w_fo_b, j, h, g1, S, tm, og=qg, og_block=1)
        i = l // 2
        if l % 2 == 0:
            h = _ffn_dense(h, ng2, sh2, sc2, g2, w_fg_b, w_fu_b, w_fd_b, i, S, tm_ffn, tf_dense)
        else:
            rw = jnp.zeros((D, LANES), F32).at[:, :n_e].set(router_w[i]).astype(BF16)
            rb = jnp.zeros((1, LANES), F32).at[0, :n_e].set(router_b[i])
            m, idx8, wt8 = _router(h, ng2, sh2, sc2, rw, rb, S, tm, n_e)
            order, pos, units = _dispatch_tables(idx8[:, :TOP_K], n_e, tm_moe)
            xs = m.at[order // TOP_K].get(mode="promise_in_bounds")
            ys = _ffn_moe(units, xs, moe_w_gate, moe_w_up, moe_w_down, i, tm_moe, tf_moe)
            ya = ys.at[pos[:, 0]].get(mode="promise_in_bounds")
            yb = ys.at[pos[:, 1]].get(mode="promise_in_bounds")
            h = _moe_combine(h, ya, yb, wt8, g2, S, tm, final=final if l == depth - 1 else None)

    if (depth - 1) % 2 == 0:
        h = _final_norm(h, *final, S, tm)
    return h.reshape(B, S, D)
```

```python
import functools
import math

import jax
import jax.numpy as jnp
from jax import lax
from jax.experimental import pallas as pl
from jax.experimental.pallas import tpu as pltpu

F32 = jnp.float32
BF16 = jnp.bfloat16

CHUNK = 64
RET_HEADS = 4
FOX_HEADS = 16
ROPE_BASE = 10000.0
TOP_K = 2
EPS = 1e-6

LANES = 128
MXU_DIM = 256
VMEM_LIMIT_BYTES = 56 * 1024 * 1024
NEG_BIG = -1e30
DV_ROWS = 80
SKIP_LOG2 = 160.0
NORM_SLACK = 1.02
LOG2E = math.log2(math.e)


def _params(*sem):
    return pltpu.CompilerParams(dimension_semantics=sem, vmem_limit_bytes=VMEM_LIMIT_BYTES)


def _silu(x):
    return x * (1.0 / (1.0 + jnp.exp(-x)))


def _sigmoid(x):
    return 1.0 / (1.0 + jnp.exp(-x))


def _norm_mod(x, g, sh, sc):
    var = jnp.mean(x * x, axis=-1, keepdims=True)
    y = (x * lax.rsqrt(var + EPS)) * g
    return y * (1.0 + sc) + sh


def _ada_kernel(c_ref, w_ref, b_ref, o_ref):
    ca = _silu(c_ref[...]).astype(BF16)
    o_ref[...] = jnp.dot(ca, w_ref[...].astype(BF16), preferred_element_type=F32) + b_ref[...]


def _ada(c_pad, w, b, tn=1024):
    L, D, N = w.shape
    return pl.pallas_call(
        _ada_kernel,
        grid=(L, N // tn),
        in_specs=[
            pl.BlockSpec((8, D), lambda l, j: (0, 0)),
            pl.BlockSpec((None, D, tn), lambda l, j: (l, 0, j)),
            pl.BlockSpec((None, 1, tn), lambda l, j: (l, 0, j)),
        ],
        out_specs=pl.BlockSpec((None, 8, tn), lambda l, j: (l, 0, j)),
        out_shape=jax.ShapeDtypeStruct((L, 8, N), F32),
        compiler_params=_params("parallel", "parallel"),
        name="ada_proj",
    )(c_pad, w, b)


def _nmm_kernel(*refs, n_chunks, cw, modes, scales, out_dtype):
    h_ref, g_ref, sh_ref, sc_ref, w_ref = refs[:5]
    has_rot = any(m == "rot" for m in modes)
    if has_rot:
        cos_ref, sin_ref, o_ref = refs[5:8]
    else:
        o_ref = refs[5]
    a = _norm_mod(h_ref[...], g_ref[...], sh_ref[...], sc_ref[...]).astype(BF16)
    for c in range(n_chunks):
        acc = jnp.dot(a, w_ref[:, c * cw:(c + 1) * cw], preferred_element_type=F32)
        if modes[c] == "rot":
            cos = cos_ref[...]
            sin = sin_ref[...]
            parts = []
            for hh in range(cw // (2 * LANES)):
                x1 = acc[:, (2 * hh) * LANES:(2 * hh + 1) * LANES]
                x2 = acc[:, (2 * hh + 1) * LANES:(2 * hh + 2) * LANES]
                parts.append(x1 * cos - x2 * sin)
                parts.append(x2 * cos + x1 * sin)
            acc = jnp.concatenate(parts, axis=-1)
        if scales[c] != 1.0:
            acc = acc * scales[c]
        o_ref[:, c * cw:(c + 1) * cw] = acc.astype(out_dtype)


def _norm_mod_matmul(h, g, sh, sc, w, S, *, modes, scales, cw, out_dtype, tm, rot=None, w_layer=0,
                     w_col=0):
    T, D = h.shape
    n_chunks = len(modes)
    N = n_chunks * cw
    per_b = S // tm
    in_specs = [
        pl.BlockSpec((tm, D), lambda i: (i, 0)),
        pl.BlockSpec((1, D), lambda i: (0, 0)),
        pl.BlockSpec((None, 1, D), lambda i: (i // per_b, 0, 0)),
        pl.BlockSpec((None, 1, D), lambda i: (i // per_b, 0, 0)),
        pl.BlockSpec((None, D, N), lambda i: (w_layer, 0, w_col)),
    ]
    args = [h, g, sh, sc, w]
    if rot is not None:
        in_specs += [pl.BlockSpec((tm, LANES), lambda i: (i % per_b, 0))] * 2
        args += list(rot)
    return pl.pallas_call(
        functools.partial(_nmm_kernel, n_chunks=n_chunks, cw=cw, modes=modes, scales=scales,
                          out_dtype=out_dtype),
        grid=(T // tm,),
        in_specs=in_specs,
        out_specs=pl.BlockSpec((tm, N), lambda i: (i, 0)),
        out_shape=jax.ShapeDtypeStruct((T, N), out_dtype),
        compiler_params=_params("parallel"),
        name="norm_mod_matmul",
    )(*args)


def _ret_kernel(q_ref, k_ref, v_ref, g_ref, intra_ref, qd_ref, kd_ref, cd_ref, o_ref, st_ref, *,
                n_sub, dk, dv):
    @pl.when(pl.program_id(0) == 0)
    def _():
        st_ref[...] = jnp.zeros_like(st_ref)

    def body(c, carry):
        r0 = pl.multiple_of(c * CHUNK, CHUNK)
        rows = pl.ds(r0, CHUNK)
        for b in range(q_ref.shape[0]):
            for h in range(RET_HEADS):
                qc = q_ref[b, rows, h * dk:(h + 1) * dk]
                kc = k_ref[b, rows, h * dk:(h + 1) * dk]
                vc = v_ref[b, rows, h * dv:(h + 1) * dv]
                state = st_ref[b, h]
                scores = lax.dot_general(qc, kc, (((1,), (1,)), ((), ())),
                                         preferred_element_type=F32) * intra_ref[h]
                o = (jnp.dot(scores.astype(BF16), vc, preferred_element_type=F32)
                     + jnp.dot(qc, state.astype(BF16), preferred_element_type=F32) * qd_ref[h])
                kdk = (kc.astype(F32) * kd_ref[h]).astype(BF16)
                st_ref[b, h] = state * cd_ref[h] + lax.dot_general(
                    kdk, vc, (((0,), (0,)), ((), ())), preferred_element_type=F32)
                mu = jnp.mean(o, axis=-1, keepdims=True)
                d = o - mu
                var = jnp.mean(d * d, axis=-1, keepdims=True)
                y = d * lax.rsqrt(var + EPS)
                gate = g_ref[b, rows, h * dv:(h + 1) * dv]
                o_ref[b, rows, h * dv:(h + 1) * dv] = (_silu(gate) * y).astype(o_ref.dtype)
        return carry

    lax.fori_loop(0, n_sub, body, 0)


def _retention(qkv, gate, tabs, B, S, lc):
    T = qkv.shape[0]
    D = qkv.shape[1] // 4
    dk = D // RET_HEADS
    dv = 2 * D // RET_HEADS
    H = RET_HEADS
    intra, qd, kd, cd = tabs
    qkv3 = qkv.reshape(B, S, 4 * D)
    gate3 = gate.reshape(B, S, 2 * D)
    out = pl.pallas_call(
        functools.partial(_ret_kernel, n_sub=lc // CHUNK, dk=dk, dv=dv),
        grid=(S // lc,),
        in_specs=[
            pl.BlockSpec((B, lc, D), lambda s: (0, s, 0)),
            pl.BlockSpec((B, lc, D), lambda s: (0, s, 1)),
            pl.BlockSpec((B, lc, 2 * D), lambda s: (0, s, 1)),
            pl.BlockSpec((B, lc, 2 * D), lambda s: (0, s, 0)),
            pl.BlockSpec((H, CHUNK, CHUNK), lambda s: (0, 0, 0)),
            pl.BlockSpec((H, CHUNK, 1), lambda s: (0, 0, 0)),
            pl.BlockSpec((H, CHUNK, 1), lambda s: (0, 0, 0)),
            pl.BlockSpec((H, 1, 1), lambda s: (0, 0, 0)),
        ],
        out_specs=pl.BlockSpec((B, lc, 2 * D), lambda s: (0, s, 0)),
        out_shape=jax.ShapeDtypeStruct((B, S, 2 * D), BF16),
        scratch_shapes=[pltpu.VMEM((B, H, dk, dv), F32)],
        compiler_params=_params("arbitrary"),
        name="retention",
    )(qkv3, qkv3, qkv3, gate3, intra, qd, kd, cd)
    return out.reshape(T, 2 * D)


def _retention_tables():
    h = jnp.arange(RET_HEADS, dtype=F32)
    log_g = jnp.log(1.0 - 2.0 ** (-5.0 - h))
    idx = jnp.arange(CHUNK, dtype=F32)
    intra = jnp.exp(log_g[:, None, None] * jnp.abs(idx[:, None] - idx[None, :]))
    qd = jnp.exp(log_g[:, None] * (idx[None, :] + 1.0))[:, :, None]
    kd = jnp.exp(log_g[:, None] * (CHUNK - 1.0 - idx[None, :]))[:, :, None]
    cd = jnp.exp(log_g * CHUNK)[:, None, None]
    return intra, qd, kd, cd


def _rotary_tables(S, d):
    inv = 1.0 / (ROPE_BASE ** (jnp.arange(0, d, 2, dtype=F32) / d))
    ang = jnp.arange(S).astype(F32)[:, None] * inv[None, :]
    return jnp.cos(ang), jnp.sin(ang)


def _proj_res_kernel(*refs, gated):
    if gated:
        o_in, og_ref, w_ref, h_ref, gt_ref, out_ref = refs
        x = (o_in[...] * _sigmoid(og_ref[...])).astype(BF16)
    else:
        x_ref, w_ref, h_ref, gt_ref, out_ref = refs
        x = x_ref[...]
    out_ref[...] = h_ref[...] + gt_ref[...] * jnp.dot(x, w_ref[...], preferred_element_type=F32)


def _proj_residual(x, w, w_layer, h, gate, S, tm, og=None, og_block=0):
    T, D = h.shape
    K = w.shape[1]
    per_b = S // tm
    gated = og is not None
    xs = (x, og) if gated else (x,)
    in_specs = [pl.BlockSpec((tm, K), lambda i: (i, 0))]
    if gated:
        in_specs.append(pl.BlockSpec((tm, K), lambda i: (i, og_block)))
    in_specs += [
        pl.BlockSpec((None, K, D), lambda i: (w_layer, 0, 0)),
        pl.BlockSpec((tm, D), lambda i: (i, 0)),
        pl.BlockSpec((None, 1, D), lambda i: (i // per_b, 0, 0)),
    ]
    return pl.pallas_call(
        functools.partial(_proj_res_kernel, gated=gated),
        grid=(T // tm,),
        in_specs=in_specs,
        out_specs=pl.BlockSpec((tm, D), lambda i: (i, 0)),
        out_shape=jax.ShapeDtypeStruct((T, D), F32),
        compiler_params=_params("parallel"),
        name="proj_residual",
    )(*xs, w, h, gate)


def _ffn_dense_kernel(h_ref, g_ref, sh_ref, sc_ref, gt_ref, wg_ref, wu_ref, wd_ref, o_ref):
    h = h_ref[...]
    a = _norm_mod(h, g_ref[...], sh_ref[...], sc_ref[...]).astype(BF16)
    gate = jnp.dot(a, wg_ref[...], preferred_element_type=F32)
    up = jnp.dot(a, wu_ref[...], preferred_element_type=F32)
    act = (_silu(gate) * up).astype(BF16)
    o_ref[...] = h + gt_ref[...] * jnp.dot(act, wd_ref[...], preferred_element_type=F32)


def _ffn_dense(h, g, sh, sc, gate, wg, wu, wd, layer, S, tm):
    T, D = h.shape
    F = wg.shape[2]
    per_b = S // tm
    once = pl.Buffered(1)
    return pl.pallas_call(
        _ffn_dense_kernel,
        grid=(T // tm,),
        in_specs=[
            pl.BlockSpec((tm, D), lambda i: (i, 0)),
            pl.BlockSpec((1, D), lambda i: (0, 0)),
            pl.BlockSpec((None, 1, D), lambda i: (i // per_b, 0, 0)),
            pl.BlockSpec((None, 1, D), lambda i: (i // per_b, 0, 0)),
            pl.BlockSpec((None, 1, D), lambda i: (i // per_b, 0, 0)),
            pl.BlockSpec((None, D, F), lambda i: (layer, 0, 0), pipeline_mode=once),
            pl.BlockSpec((None, D, F), lambda i: (layer, 0, 0), pipeline_mode=once),
            pl.BlockSpec((None, F, D), lambda i: (layer, 0, 0), pipeline_mode=once),
        ],
        out_specs=pl.BlockSpec((tm, D), lambda i: (i, 0)),
        out_shape=jax.ShapeDtypeStruct((T, D), F32),
        compiler_params=_params("parallel"),
        name="ffn_dense",
    )(h, g, sh, sc, gate, wg, wu, wd)


def _ffn_moe_kernel(ut_ref, ue_ref, ulo_ref, uhi_ref, ufirst_ref, uvalid_ref, x_ref, wg_ref, wu_ref,
                    wd_ref, o_ref, acc_scr):
    u = pl.program_id(0)
    f = pl.program_id(1)
    valid = uvalid_ref[u] > 0

    @pl.when(jnp.logical_and(jnp.logical_and(valid, ufirst_ref[u] > 0), f == 0))
    def _():
        acc_scr[...] = jnp.zeros_like(acc_scr)

    tm = x_ref.shape[0]
    half = tm // 2
    lo = ulo_ref[u]
    hi = uhi_ref[u]
    whole = jnp.logical_and(lo == 0, hi == tm)

    def swiglu(rows, r0, masked):
        x = x_ref[rows, :]
        gate = jnp.dot(x, wg_ref[...].astype(BF16), preferred_element_type=F32)
        up = jnp.dot(x, wu_ref[...].astype(BF16), preferred_element_type=F32)
        act = _silu(gate) * up
        if masked:
            row = r0 + lax.broadcasted_iota(jnp.int32, gate.shape, 0)
            act = jnp.where(jnp.logical_and(row >= lo, row < hi), act, 0.0)
        acc_scr[rows, :] += jnp.dot(act.astype(BF16), wd_ref[...].astype(BF16),
                                    preferred_element_type=F32)

    @pl.when(jnp.logical_and(valid, whole))
    def _():
        swiglu(slice(None), 0, False)

    for k in range(2):
        touches = jnp.logical_and(lo < (k + 1) * half, hi > k * half)

        @pl.when(jnp.logical_and(jnp.logical_and(valid, jnp.logical_not(whole)), touches))
        def _(k=k):
            swiglu(slice(k * half, (k + 1) * half), k * half, True)

    @pl.when(jnp.logical_and(valid, f == pl.num_programs(1) - 1))
    def _():
        o_ref[...] = acc_scr[...]


def _ffn_moe(units, xs, wg, wu, wd, layer, tm, tf):
    N, D = xs.shape
    F = wg.shape[3]
    nf = F // tf
    n_units = units[0].shape[0]

    def xrow(u, f, ut, ue, ulo, uhi, ufirst, uvalid):
        return (ut[u], 0)

    def wcol(u, f, ut, ue, ulo, uhi, ufirst, uvalid):
        return (layer, ue[u], 0, jnp.where(uvalid[u] > 0, f, nf - 1))

    def wrow(u, f, ut, ue, ulo, uhi, ufirst, uvalid):
        return (layer, ue[u], jnp.where(uvalid[u] > 0, f, nf - 1), 0)

    grid_spec = pltpu.PrefetchScalarGridSpec(
        num_scalar_prefetch=6,
        grid=(n_units, nf),
        in_specs=[
            pl.BlockSpec((tm, D), xrow),
            pl.BlockSpec((None, None, D, tf), wcol),
            pl.BlockSpec((None, None, D, tf), wcol),
            pl.BlockSpec((None, None, tf, D), wrow),
        ],
        out_specs=pl.BlockSpec((tm, D), xrow),
        scratch_shapes=[pltpu.VMEM((tm, D), F32)],
    )
    return pl.pallas_call(
        _ffn_moe_kernel,
        grid_spec=grid_spec,
        out_shape=jax.ShapeDtypeStruct((N, D), F32),
        compiler_params=_params("arbitrary", "arbitrary"),
        name="ffn_moe",
    )(*units, xs, wg, wu, wd)


def _combine_kernel(*refs, fused_final):
    h_ref, ya_ref, yb_ref, wt_ref, gt_ref = refs[:5]
    o_ref = refs[-1]
    wt = wt_ref[...]
    ff = wt[:, 0:1] * ya_ref[...] + wt[:, 1:2] * yb_ref[...]
    out = h_ref[...] + gt_ref[...] * ff
    if fused_final:
        g_ref, sh_ref, sc_ref = refs[5:8]
        out = _norm_mod(out, g_ref[...], sh_ref[...], sc_ref[...])
    o_ref[...] = out


def _moe_combine(h, ya, yb, wt8, gate, S, tm, final=None):
    T, D = h.shape
    per_b = S // tm
    row = pl.BlockSpec((tm, D), lambda i: (i, 0))
    per_batch = pl.BlockSpec((None, 1, D), lambda i: (i // per_b, 0, 0))
    in_specs = [row, row, row, pl.BlockSpec((tm, 8), lambda i: (i, 0)), per_batch]
    args = [h, ya, yb, wt8, gate]
    if final is not None:
        in_specs += [pl.BlockSpec((1, D), lambda i: (0, 0)), per_batch, per_batch]
        args += list(final)
    return pl.pallas_call(
        functools.partial(_combine_kernel, fused_final=final is not None),
        grid=(T // tm,),
        in_specs=in_specs,
        out_specs=row,
        out_shape=jax.ShapeDtypeStruct((T, D), F32),
        compiler_params=_params("parallel"),
        name="moe_combine",
    )(*args)


def _router_kernel(h_ref, g_ref, sh_ref, sc_ref, rw_ref, rb_ref, m_ref, idx_ref, wt_ref, *, n_e):
    a = _norm_mod(h_ref[...], g_ref[...], sh_ref[...], sc_ref[...]).astype(BF16)
    m_ref[...] = a
    logits = jnp.dot(a, rw_ref[...], preferred_element_type=F32) + rb_ref[...]
    lane = lax.broadcasted_iota(jnp.int32, logits.shape, 1)
    lane_f = lane.astype(F32)
    logits = jnp.where(lane < n_e, logits, NEG_BIG)
    m1 = jnp.max(logits, axis=-1, keepdims=True)
    i1 = jnp.min(jnp.where(logits == m1, lane_f, float(LANES)), axis=-1, keepdims=True)
    rest = jnp.where(lane_f == i1, NEG_BIG, logits)
    m2 = jnp.max(rest, axis=-1, keepdims=True)
    i2 = jnp.min(jnp.where(rest == m2, lane_f, float(LANES)), axis=-1, keepdims=True)
    e2 = jnp.exp(m2 - m1)
    w1 = 1.0 / (1.0 + e2)
    w2 = e2 * w1
    w = jnp.where(lane == 0, w1, jnp.where(lane == 1, w2, 0.0))
    ii = jnp.where(lane == 0, i1, jnp.where(lane == 1, i2, 0.0)).astype(jnp.int32)
    wt_ref[...] = w[:, :8]
    idx_ref[...] = ii[:, :8]


def _router(h, g, sh, sc, rw, rb, S, tm, n_e):
    T, D = h.shape
    per_b = S // tm
    return pl.pallas_call(
        functools.partial(_router_kernel, n_e=n_e),
        grid=(T // tm,),
        in_specs=[
            pl.BlockSpec((tm, D), lambda i: (i, 0)),
            pl.BlockSpec((1, D), lambda i: (0, 0)),
            pl.BlockSpec((None, 1, D), lambda i: (i // per_b, 0, 0)),
            pl.BlockSpec((None, 1, D), lambda i: (i // per_b, 0, 0)),
            pl.BlockSpec((D, LANES), lambda i: (0, 0)),
            pl.BlockSpec((1, LANES), lambda i: (0, 0)),
        ],
        out_specs=[
            pl.BlockSpec((tm, D), lambda i: (i, 0)),
            pl.BlockSpec((tm, 8), lambda i: (i, 0)),
            pl.BlockSpec((tm, 8), lambda i: (i, 0)),
        ],
        out_shape=[
            jax.ShapeDtypeStruct((T, D), BF16),
            jax.ShapeDtypeStruct((T, 8), jnp.int32),
            jax.ShapeDtypeStruct((T, 8), F32),
        ],
        compiler_params=_params("parallel"),
        name="router",
    )(h, g, sh, sc, rw, rb)


def _forget_kernel(h_ref, g_ref, sh_ref, sc_ref, wf_ref, bf_ref, o_ref, carry_ref, *, per_b):
    @pl.when(pl.program_id(0) % per_b == 0)
    def _():
        carry_ref[...] = jnp.zeros_like(carry_ref)

    a = _norm_mod(h_ref[...], g_ref[...], sh_ref[...], sc_ref[...]).astype(BF16)
    z = jnp.dot(a, wf_ref[...], preferred_element_type=F32) + bf_ref[...]
    lf = jnp.minimum(z, 0.0) - jnp.log(1.0 + jnp.exp(-jnp.abs(z)))
    tm = lf.shape[0]
    r = lax.broadcasted_iota(jnp.int32, (tm, tm), 0)
    c = lax.broadcasted_iota(jnp.int32, (tm, tm), 1)
    tri = jnp.where(c <= r, 1.0, 0.0).astype(F32)
    cs = jnp.dot(tri, lf, preferred_element_type=F32, precision=lax.Precision.HIGHEST)
    out = cs + carry_ref[...]
    carry_ref[...] = out[tm - 1:tm, :]
    out2 = out * LOG2E
    hi = out2.astype(BF16)
    r1 = out2 - hi.astype(F32)
    mid = r1.astype(BF16)
    lo = (r1 - mid.astype(F32)).astype(BF16)
    o_ref[:, 0:LANES] = hi
    o_ref[:, LANES:2 * LANES] = mid
    o_ref[:, 2 * LANES:3 * LANES] = lo


def _forget_cumsum(h, g, sh, sc, wf, bf, S, tm):
    T, D = h.shape
    per_b = S // tm
    return pl.pallas_call(
        functools.partial(_forget_kernel, per_b=per_b),
        grid=(T // tm,),
        in_specs=[
            pl.BlockSpec((tm, D), lambda i: (i, 0)),
            pl.BlockSpec((1, D), lambda i: (0, 0)),
            pl.BlockSpec((None, 1, D), lambda i: (i // per_b, 0, 0)),
            pl.BlockSpec((None, 1, D), lambda i: (i // per_b, 0, 0)),
            pl.BlockSpec((D, LANES), lambda i: (0, 0)),
            pl.BlockSpec((1, LANES), lambda i: (0, 0)),
        ],
        out_specs=pl.BlockSpec((tm, 3 * LANES), lambda i: (i, 0)),
        out_shape=jax.ShapeDtypeStruct((T, 3 * LANES), BF16),
        scratch_shapes=[pltpu.VMEM((1, LANES), F32)],
        compiler_params=_params("arbitrary"),
        name="forget_cumsum",
    )(h, g, sh, sc, wf, bf)


def _fox_prep_kernel(kp_ref, vp_ref, f_ref, pm_ref, ka_ref, vt_ref, kn_ref, *, dh, tk):
    c = pl.program_id(2)
    n_sub = kp_ref.shape[0] // tk
    lane = lax.broadcasted_iota(jnp.int32, (tk, LANES), 1)
    lane8 = lax.broadcasted_iota(jnp.int32, (8, LANES), 1)
    row8 = lax.broadcasted_iota(jnp.int32, (8, LANES), 0)
    row16 = lax.broadcasted_iota(jnp.int32, (DV_ROWS - dh, tk), 0)
    tail = jnp.where(row16 == 0, 1.0, 0.0).astype(BF16)

    @pl.when(c == 0)
    def _():
        kn_ref[...] = jnp.zeros_like(kn_ref)

    for j in range(n_sub):
        rows = slice(j * tk, (j + 1) * tk)
        kp = kp_ref[rows, :]
        ex = jnp.dot(f_ref[rows, :], pm_ref[...], preferred_element_type=F32)
        ex = jnp.where((lane % dh) < 3, 1.0, ex).astype(BF16)
        ka_ref[0, rows, :] = jnp.where(lane < dh, kp, ex)
        ka_ref[1, rows, :] = jnp.where(lane < dh, ex, kp)
        vt = vp_ref[rows, :].astype(F32).T
        vt_ref[0, 0:dh, rows] = vt[0:dh].astype(BF16)
        vt_ref[1, 0:dh, rows] = vt[dh:2 * dh].astype(BF16)
        vt_ref[0, dh:DV_ROWS, rows] = tail
        vt_ref[1, dh:DV_ROWS, rows] = tail
        k2 = kp.astype(F32)
        k2 = k2 * k2
        na = jnp.max(jnp.sum(jnp.where(lane < dh, k2, 0.0), axis=1, keepdims=True), axis=0, keepdims=True)
        nb = jnp.max(jnp.sum(jnp.where(lane < dh, 0.0, k2), axis=1, keepdims=True), axis=0, keepdims=True)
        upd = jnp.where(row8 == 0, na, jnp.where(row8 == 1, nb, 0.0))
        kn_ref[...] = jnp.where(lane8 == c * n_sub + j, upd, kn_ref[...])


def _fox_prep(kv, fparts, pm, B, S, D, dh, tk, tp):
    H = D // dh
    nb = S // tp
    assert S // tk <= LANES and tp % tk == 0
    return pl.pallas_call(
        functools.partial(_fox_prep_kernel, dh=dh, tk=tk),
        grid=(B, H // 2, nb),
        in_specs=[
            pl.BlockSpec((tp, LANES), lambda b, p, c: (b * nb + c, p)),
            pl.BlockSpec((tp, LANES), lambda b, p, c: (b * nb + c, D // LANES + p)),
            pl.BlockSpec((tp, 3 * LANES), lambda b, p, c: (b * nb + c, 0)),
            pl.BlockSpec((None, 3 * LANES, LANES), lambda b, p, c: (p, 0, 0)),
        ],
        out_specs=[
            pl.BlockSpec((None, 2, tp, LANES), lambda b, p, c: (b, p, c, 0)),
            pl.BlockSpec((None, 2, DV_ROWS, tp), lambda b, p, c: (b, p, 0, c)),
            pl.BlockSpec((None, None, 8, LANES), lambda b, p, c: (b, p, 0, 0)),
        ],
        out_shape=[
            jax.ShapeDtypeStruct((B, H, S, LANES), BF16),
            jax.ShapeDtypeStruct((B, H, DV_ROWS, S), BF16),
            jax.ShapeDtypeStruct((B, H // 2, 8, LANES), F32),
        ],
        compiler_params=_params("parallel", "parallel", "arbitrary"),
        name="fox_prep",
    )(kv, kv, fparts, pm)


def _fox_kernel(q_ref, gq_ref, gb_ref, kn_ref, ka_ref, vt_ref, o_ref, sa0, sa1, sb0, sb1, *,
                tq, tk, dh):
    i = pl.program_id(2)
    qT = q_ref[...].T
    g16 = gq_ref[...]
    row8 = lax.broadcasted_iota(jnp.int32, (8, tq), 0)
    xa = jnp.where(row8 < 3, g16[0:8], jnp.where(row8 < 6, 1.0, 0.0))
    xb = jnp.where(row8 < 3, g16[8:16], jnp.where(row8 < 6, 1.0, 0.0))
    zpad = jnp.zeros((LANES - dh - 8, tq), F32)
    qa = jnp.concatenate([qT[0:dh], xa, zpad], axis=0).astype(BF16)
    qb = jnp.concatenate([xb, zpad, qT[dh:2 * dh]], axis=0).astype(BF16)

    def scores(h, q, blk):
        k0 = pl.multiple_of(blk * tk, tk)
        return jnp.dot(ka_ref[h, pl.ds(k0, tk), :], q, preferred_element_type=F32)

    def update(s_ref, h, blk, m, acc):
        k0 = pl.multiple_of(blk * tk, tk)
        s = s_ref[...]
        m_new = jnp.maximum(m, jnp.max(s, axis=0, keepdims=True))
        alpha = jnp.exp2(m - m_new)
        p = jnp.exp2(s - m_new).astype(BF16)
        pv = jnp.dot(vt_ref[h, :, pl.ds(k0, tk)], p, preferred_element_type=F32)
        return m_new, alpha * acc + pv

    q2 = qT * qT
    qna = jnp.max(jnp.sum(q2[0:dh], axis=0, keepdims=True), axis=1, keepdims=True)
    qnb = jnp.max(jnp.sum(q2[dh:2 * dh], axis=0, keepdims=True), axis=1, keepdims=True)
    kn = kn_ref[...]
    gb = gb_ref[...]
    lane = lax.broadcasted_iota(jnp.int32, (1, LANES), 1)
    at_i = lane == i

    def pick(row):
        return jnp.sum(jnp.where(at_i, row, 0.0), axis=1, keepdims=True)

    def needed(qn, knr, glast, gfirst):
        bound = NORM_SLACK * jnp.sqrt(qn) * (jnp.sqrt(knr) + jnp.sqrt(pick(knr))) + pick(gfirst) - glast
        return bound > -SKIP_LOG2

    need = jnp.logical_or(needed(qna, kn[0:1], gb[0:1], gb[2:3]), needed(qnb, kn[1:2], gb[1:2], gb[3:4]))
    lane_f = lane.astype(F32)
    i_f = i.astype(F32)
    cand = jnp.where(jnp.logical_and(need, lane < i), lane_f, i_f)
    jmin = jnp.min(cand).astype(jnp.int32)

    key = lax.broadcasted_iota(jnp.int32, (tk, tq), 0)
    qry = lax.broadcasted_iota(jnp.int32, (tk, tq), 1)
    sa0[...] = jnp.where(key <= qry, scores(0, qa, i), NEG_BIG)
    sb0[...] = jnp.where(key <= qry, scores(1, qb, i), NEG_BIG)
    ma0 = jnp.max(sa0[...], axis=0, keepdims=True)
    mb0 = jnp.max(sb0[...], axis=0, keepdims=True)

    prev = jnp.maximum(i - 1, 0)
    sa1[...] = jnp.where(i > 0, scores(0, qa, prev), NEG_BIG)
    sb1[...] = jnp.where(i > 0, scores(1, qb, prev), NEG_BIG)

    def first(s_ref, h, m0):
        k0 = pl.multiple_of(i * tk, tk)
        p = jnp.exp2(s_ref[...] - m0).astype(BF16)
        return jnp.dot(vt_ref[h, :, pl.ds(k0, tk)], p, preferred_element_type=F32)

    acca0 = first(sa0, 0, ma0)
    accb0 = first(sb0, 1, mb0)
    ma1, acca1 = update(sa1, 0, prev, ma0, acca0)
    mb1, accb1 = update(sb1, 1, prev, mb0, accb0)

    n_off = jnp.maximum(i - 1 - jmin, 0)
    n_pairs = jnp.maximum((n_off + 1) // 2, 1)

    def blk_at(pos):
        return jnp.minimum(jmin + pos, i)

    def masked_scores(h, q, pos):
        return jnp.where(pos < n_off, scores(h, q, blk_at(pos)), NEG_BIG)

    sa0[...] = masked_scores(0, qa, 0)
    sb0[...] = masked_scores(1, qb, 0)

    def pair(u, carry):
        ma, acca, mb, accb = carry
        b0 = jmin + 2 * u
        sa1[...] = scores(0, qa, b0 + 1)
        sb1[...] = scores(1, qb, b0 + 1)
        ma, acca = update(sa0, 0, b0, ma, acca)
        mb, accb = update(sb0, 1, b0, mb, accb)
        sa0[...] = scores(0, qa, b0 + 2)
        sb0[...] = scores(1, qb, b0 + 2)
        ma, acca = update(sa1, 0, b0 + 1, ma, acca)
        mb, accb = update(sb1, 1, b0 + 1, mb, accb)
        return ma, acca, mb, accb

    carry = lax.fori_loop(0, n_pairs - 1, pair, (ma1, acca1, mb1, accb1))
    u = n_pairs - 1

    def tail_two(carry):
        ma, acca, mb, accb = carry
        sa1[...] = scores(0, qa, blk_at(2 * u + 1))
        sb1[...] = scores(1, qb, blk_at(2 * u + 1))
        ma, acca = update(sa0, 0, blk_at(2 * u), ma, acca)
        mb, accb = update(sb0, 1, blk_at(2 * u), mb, accb)
        ma, acca = update(sa1, 0, blk_at(2 * u + 1), ma, acca)
        mb, accb = update(sb1, 1, blk_at(2 * u + 1), mb, accb)
        return ma, acca, mb, accb

    def tail_one(carry):
        ma, acca, mb, accb = carry
        ma, acca = update(sa0, 0, blk_at(2 * u), ma, acca)
        mb, accb = update(sb0, 1, blk_at(2 * u), mb, accb)
        return ma, acca, mb, accb

    ma, acca, mb, accb = lax.cond(2 * u + 1 < n_off, tail_two, tail_one, carry)
    oa = acca[:dh, :] * (1.0 / acca[dh:dh + 1, :])
    ob = accb[:dh, :] * (1.0 / accb[dh:dh + 1, :])
    o_ref[...] = jnp.concatenate([oa, ob], axis=0).T


def _fox_attention(qg, gq, gb, kn, ka, vt, B, S, D, dh, tq):
    T = B * S
    H = D // dh
    nq = S // tq
    return pl.pallas_call(
        functools.partial(_fox_kernel, tq=tq, tk=tq, dh=dh),
        grid=(B, H // 2, nq),
        in_specs=[
            pl.BlockSpec((tq, LANES), lambda b, p, i: (b * nq + i, p)),
            pl.BlockSpec((None, None, 16, tq), lambda b, p, i: (b, p, 0, i)),
            pl.BlockSpec((None, None, 8, LANES), lambda b, p, i: (b, p, 0, 0)),
            pl.BlockSpec((None, None, 8, LANES), lambda b, p, i: (b, p, 0, 0)),
            pl.BlockSpec((None, 2, S, LANES), lambda b, p, i: (b, p, 0, 0)),
            pl.BlockSpec((None, 2, DV_ROWS, S), lambda b, p, i: (b, p, 0, 0)),
        ],
        out_specs=pl.BlockSpec((tq, LANES), lambda b, p, i: (b * nq + i, p)),
        out_shape=jax.ShapeDtypeStruct((T, D), F32),
        scratch_shapes=[pltpu.VMEM((tq, tq), F32) for _ in range(4)],
        compiler_params=_params("parallel", "parallel", "arbitrary"),
        name="fox_attention",
    )(qg, gq, gb, kn, ka, vt)


def _fox_gate_tables(fparts, B, S, H, tk):
    f3 = fparts.reshape(B, S, 3, LANES)[:, :, :, :H].astype(F32)
    gq = f3.reshape(B, S, 3, H // 2, 2).transpose(0, 3, 4, 2, 1)
    gq = jnp.pad(gq, ((0, 0), (0, 0), (0, 0), (0, 5), (0, 0))).reshape(B, H // 2, 16, S)
    G = jnp.sum(f3, axis=2)
    nb = S // tk
    Gb = G.reshape(B, nb, tk, H // 2, 2)
    last = Gb[:, :, tk - 1].transpose(0, 2, 3, 1)
    first = Gb[:, :, 0].transpose(0, 2, 3, 1)
    gb = jnp.concatenate([last, first, jnp.zeros_like(last), jnp.zeros_like(last)], axis=2)
    gb = jnp.pad(gb, ((0, 0), (0, 0), (0, 0), (0, LANES - nb)))
    return gq, gb


def _fox_placement(H, dh):
    p = jnp.arange(H // 2)[:, None, None]
    r = jnp.arange(3 * LANES)[None, :, None]
    l = jnp.arange(LANES)[None, None, :]
    j = r // LANES
    head = r % LANES
    hit_b = jnp.logical_and(head == 2 * p + 1, l == 3 + j)
    hit_a = jnp.logical_and(head == 2 * p, l == dh + 3 + j)
    return jnp.where(jnp.logical_or(hit_a, hit_b), -1.0, 0.0).astype(BF16)


def _final_kernel(h_ref, g_ref, sh_ref, sc_ref, o_ref):
    o_ref[...] = _norm_mod(h_ref[...], g_ref[...], sh_ref[...], sc_ref[...])


def _final_norm(h, g, sh, sc, S, tm):
    T, D = h.shape
    per_b = S // tm
    return pl.pallas_call(
        _final_kernel,
        grid=(T // tm,),
        in_specs=[
            pl.BlockSpec((tm, D), lambda i: (i, 0)),
            pl.BlockSpec((1, D), lambda i: (0, 0)),
            pl.BlockSpec((None, 1, D), lambda i: (i // per_b, 0, 0)),
            pl.BlockSpec((None, 1, D), lambda i: (i // per_b, 0, 0)),
        ],
        out_specs=pl.BlockSpec((tm, D), lambda i: (i, 0)),
        out_shape=jax.ShapeDtypeStruct((T, D), F32),
        compiler_params=_params("parallel"),
        name="final_norm",
    )(h, g, sh, sc)


def _dispatch_tables(top_idx, n_e, tm):
    T, K = top_idx.shape
    n = T * K
    n_tiles = n // tm
    n_units = n_tiles + n_e
    e_flat = top_idx.reshape(n)
    order = jnp.argsort(e_flat, stable=True).astype(jnp.int32)
    pos = jnp.argsort(order).astype(jnp.int32)
    ids = jnp.arange(n_e, dtype=jnp.int32)
    counts = jnp.sum(e_flat[:, None] == ids[None, :], axis=0).astype(jnp.int32)
    ends = jnp.cumsum(counts)
    starts = ends - counts
    t0 = jnp.arange(n_tiles, dtype=jnp.int32)[:, None] * tm
    present = jnp.logical_and(starts[None, :] < t0 + tm, ends[None, :] > t0)
    present = jnp.logical_and(present, counts[None, :] > 0)
    flat = jnp.arange(n_tiles * n_e, dtype=jnp.int32)
    keys = jnp.sort(jnp.where(present.reshape(-1), flat, n_tiles * n_e))[:n_units]
    valid = keys < n_tiles * n_e
    n_valid = jnp.sum(valid.astype(jnp.int32))
    keys = jnp.where(valid, keys, keys[jnp.maximum(n_valid - 1, 0)])
    ut = keys // n_e
    ue = keys % n_e
    ulo = jnp.clip(starts[ue] - ut * tm, 0, tm)
    uhi = jnp.clip(ends[ue] - ut * tm, 0, tm)
    prev_t = jnp.concatenate([jnp.full((1,), -1, jnp.int32), ut[:-1]])
    ufirst = (ut != prev_t).astype(jnp.int32)
    units = tuple(a.astype(jnp.int32) for a in (ut, ue, ulo, uhi, ufirst, valid))
    return order, pos.reshape(T, K), units


def _tile(S, want):
    t = min(S, want)
    assert S % t == 0
    return t


def _mxu_tile(F, cap):
    fits = [t for t in range(MXU_DIM, min(F, cap) + 1, MXU_DIM) if F % t == 0]
    return max(fits) if fits else F


def kernel(x, c, ada_w, ada_b, norm_g, ret_w_in, ret_w_o, kv_ada_w, kv_ada_b, kv_norm_g, fox_w_kv, fox_w_f, fox_b_f, fox_w_qg, fox_w_o, ffn_w_gate, ffn_w_up, ffn_w_down, router_w, router_b, moe_w_gate, moe_w_up, moe_w_down, final_ada_w, final_ada_b, final_norm_g):
    B, S, D = x.shape
    T = B * S
    depth = ada_w.shape[0]
    n_a = ret_w_in.shape[0]
    n_e = router_w.shape[-1]
    dh = D // FOX_HEADS
    dk = D // RET_HEADS
    assert dk == 2 * LANES and S % CHUNK == 0 and n_e <= 8

    tm = _tile(S, 512)
    tm_moe = _tile(S, 1024)
    tq = _tile(S, 512)
    lc = _tile(S, 256)
    f_moe = moe_w_gate.shape[-1]
    tf_moe = _mxu_tile(f_moe, 1024)

    c_pad = jnp.zeros((8, D), F32).at[:B].set(c)
    ada = _ada(c_pad, ada_w, ada_b[:, None, :])[:, :B]
    extra_w = jnp.stack([kv_ada_w, final_ada_w])
    extra_b = jnp.stack([kv_ada_b, final_ada_b])[:, None, :]
    extra = _ada(c_pad, extra_w, extra_b)[:, :B]
    vec = lambda a: a[:, None, :]

    h = x.reshape(T, D)
    cos, sin = _rotary_tables(S, dk)
    ret_tabs = _retention_tables()
    shared = None
    w_in_b, w_ro_b = ret_w_in.astype(BF16), ret_w_o.astype(BF16)
    w_kv_b, w_qg_b, w_fo_b = fox_w_kv.astype(BF16)[None], fox_w_qg.astype(BF16), fox_w_o.astype(BF16)
    w_fg_b, w_fu_b, w_fd_b = ffn_w_gate.astype(BF16), ffn_w_up.astype(BF16), ffn_w_down.astype(BF16)

    f_sh, f_sc = [vec(a) for a in jnp.split(extra[1], 2, axis=-1)]
    final = (final_norm_g[None, :], f_sh, f_sc)

    for l in range(depth):
        sh1, sc1, g1, sh2, sc2, g2 = [vec(a) for a in jnp.split(ada[l], 6, axis=-1)]
        ng1 = norm_g[l, 0][None, :]
        ng2 = norm_g[l, 1][None, :]
        if l == n_a:
            kv_sh, kv_sc = [vec(a) for a in jnp.split(extra[0], 2, axis=-1)]
            kvg = kv_norm_g[None, :]
            kv = _norm_mod_matmul(h, kvg, kv_sh, kv_sc, w_kv_b, S,
                                  modes=("plain",) * 4, scales=(1.0,) * 4, cw=D // 2,
                                  out_dtype=BF16, tm=tm)
            wf = jnp.zeros((D, LANES), F32).at[:, :FOX_HEADS].set(fox_w_f).astype(BF16)
            bf = jnp.zeros((1, LANES), F32).at[0, :FOX_HEADS].set(fox_b_f)
            fparts = _forget_cumsum(h, kvg, kv_sh, kv_sc, wf, bf, S, tm)
            ka, vt, kn = _fox_prep(kv, fparts, _fox_placement(FOX_HEADS, dh), B, S, D, dh, tq,
                                   _tile(S, 4 * tq))
            gq, gb = _fox_gate_tables(fparts, B, S, FOX_HEADS, tq)
            shared = (gq, gb, kn, ka, vt)
        if l < n_a:
            qkv = _norm_mod_matmul(
                h, ng1, sh1, sc1, w_in_b, S,
                modes=("rot",) * 4 + ("plain",) * 4,
                scales=(1.0,) * 2 + (dk ** -0.5,) * 2 + (1.0,) * 4,
                cw=D // 2, out_dtype=BF16, tm=tm, rot=(cos, sin), w_layer=l, w_col=0)
            gate = _norm_mod_matmul(h, ng1, sh1, sc1, w_in_b, S,
                                    modes=("plain",) * 4, scales=(1.0,) * 4, cw=D // 2,
                                    out_dtype=F32, tm=tm, w_layer=l, w_col=2)
            y = _retention(qkv, gate, ret_tabs, B, S, lc)
            h = _proj_residual(y, w_ro_b, l, h, g1, S, tm)
        else:
            j = l - n_a
            qg = _norm_mod_matmul(h, ng1, sh1, sc1, w_qg_b, S,
                                  modes=("plain",) * 4, scales=(dh ** -0.5 * LOG2E,) * 2 + (1.0,) * 2,
                                  cw=D // 2, out_dtype=F32, tm=tm, w_layer=j)
            o = _fox_attention(qg, *shared, B, S, D, dh, tq)
            h = _proj_residual(o, w_fo_b, j, h, g1, S, tm, og=qg, og_block=1)
        i = l // 2
        if l % 2 == 0:
            h = _ffn_dense(h, ng2, sh2, sc2, g2, w_fg_b, w_fu_b, w_fd_b, i, S, tm)
        else:
            rw = jnp.zeros((D, LANES), F32).at[:, :n_e].set(router_w[i]).astype(BF16)
            rb = jnp.zeros((1, LANES), F32).at[0, :n_e].set(router_b[i])
            m, idx8, wt8 = _router(h, ng2, sh2, sc2, rw, rb, S, tm, n_e)
            order, pos, units = _dispatch_tables(idx8[:, :TOP_K], n_e, tm_moe)
            xs = m.at[order // TOP_K].get(mode="promise_in_bounds")
            ys = _ffn_moe(units, xs, moe_w_gate, moe_w_up, moe_w_down, i, tm_moe, tf_moe)
            ya = ys.at[pos[:, 0]].get(mode="promise_in_bounds")
            yb = ys.at[pos[:, 1]].get(mode="promise_in_bounds")
            h = _moe_combine(h, ya, yb, wt8, g2, S, tm, final=final if l == depth - 1 else None)

    if (depth - 1) % 2 == 0:
        h = _final_norm(h, *final, S, tm)
    return h.reshape(B, S, D)
```

```python
import functools
import math

import jax
import jax.numpy as jnp
from jax import lax
from jax.experimental import pallas as pl
from jax.experimental.pallas import tpu as pltpu

F32 = jnp.float32
BF16 = jnp.bfloat16

CHUNK = 64
RET_HEADS = 4
FOX_HEADS = 16
ROPE_BASE = 10000.0
TOP_K = 2
EPS = 1e-6

LANES = 128
MXU_DIM = 256
VMEM_LIMIT_BYTES = 56 * 1024 * 1024
NEG_BIG = -1e30
DV_ROWS = 80
SKIP_LOG2 = 160.0
NORM_SLACK = 1.02
LOG2E = math.log2(math.e)


def _params(*sem):
    return pltpu.CompilerParams(dimension_semantics=sem, vmem_limit_bytes=VMEM_LIMIT_BYTES)


def _silu(x):
    return x * (1.0 / (1.0 + jnp.exp(-x)))


def _sigmoid(x):
    return 1.0 / (1.0 + jnp.exp(-x))


def _norm_mod(x, g, sh, sc):
    var = jnp.mean(x * x, axis=-1, keepdims=True)
    y = (x * lax.rsqrt(var + EPS)) * g
    return y * (1.0 + sc) + sh


def _ada_kernel(c_ref, w_ref, b_ref, o_ref):
    ca = _silu(c_ref[...]).astype(BF16)
    o_ref[...] = jnp.dot(ca, w_ref[...].astype(BF16), preferred_element_type=F32) + b_ref[...]


def _ada(c_pad, w, b, tn=1024):
    L, D, N = w.shape
    return pl.pallas_call(
        _ada_kernel,
        grid=(L, N // tn),
        in_specs=[
            pl.BlockSpec((8, D), lambda l, j: (0, 0)),
            pl.BlockSpec((None, D, tn), lambda l, j: (l, 0, j)),
            pl.BlockSpec((None, 1, tn), lambda l, j: (l, 0, j)),
        ],
        out_specs=pl.BlockSpec((None, 8, tn), lambda l, j: (l, 0, j)),
        out_shape=jax.ShapeDtypeStruct((L, 8, N), F32),
        compiler_params=_params("parallel", "parallel"),
        name="ada_proj",
    )(c_pad, w, b)


def _nmm_kernel(*refs, n_chunks, cw, modes, scales, out_dtype):
    h_ref, g_ref, sh_ref, sc_ref, w_ref = refs[:5]
    has_rot = any(m == "rot" for m in modes)
    if has_rot:
        cos_ref, sin_ref, o_ref = refs[5:8]
    else:
        o_ref = refs[5]
    a = _norm_mod(h_ref[...], g_ref[...], sh_ref[...], sc_ref[...]).astype(BF16)
    for c in range(n_chunks):
        acc = jnp.dot(a, w_ref[:, c * cw:(c + 1) * cw], preferred_element_type=F32)
        if modes[c] == "rot":
            cos = cos_ref[...]
            sin = sin_ref[...]
            parts = []
            for hh in range(cw // (2 * LANES)):
                x1 = acc[:, (2 * hh) * LANES:(2 * hh + 1) * LANES]
                x2 = acc[:, (2 * hh + 1) * LANES:(2 * hh + 2) * LANES]
                parts.append(x1 * cos - x2 * sin)
                parts.append(x2 * cos + x1 * sin)
            acc = jnp.concatenate(parts, axis=-1)
        if scales[c] != 1.0:
            acc = acc * scales[c]
        if modes[c] == "silu":
            acc = _silu(acc)
        o_ref[:, c * cw:(c + 1) * cw] = acc.astype(out_dtype)


def _norm_mod_matmul(h, g, sh, sc, w, S, *, modes, scales, cw, out_dtype, tm, rot=None, w_layer=0,
                     w_col=0):
    T, D = h.shape
    n_chunks = len(modes)
    N = n_chunks * cw
    per_b = S // tm
    in_specs = [
        pl.BlockSpec((tm, D), lambda i: (i, 0)),
        pl.BlockSpec((1, D), lambda i: (0, 0)),
        pl.BlockSpec((None, 1, D), lambda i: (i // per_b, 0, 0)),
        pl.BlockSpec((None, 1, D), lambda i: (i // per_b, 0, 0)),
        pl.BlockSpec((None, D, N), lambda i: (w_layer, 0, w_col)),
    ]
    args = [h, g, sh, sc, w]
    if rot is not None:
        in_specs += [pl.BlockSpec((tm, LANES), lambda i: (i % per_b, 0))] * 2
        args += list(rot)
    return pl.pallas_call(
        functools.partial(_nmm_kernel, n_chunks=n_chunks, cw=cw, modes=modes, scales=scales,
                          out_dtype=out_dtype),
        grid=(T // tm,),
        in_specs=in_specs,
        out_specs=pl.BlockSpec((tm, N), lambda i: (i, 0)),
        out_shape=jax.ShapeDtypeStruct((T, N), out_dtype),
        compiler_params=_params("parallel"),
        name="norm_mod_matmul",
    )(*args)


def _ret_kernel(q_ref, k_ref, v_ref, g_ref, intra_ref, qd_ref, kd_ref, cd_ref, o_ref, st_ref, *,
                n_sub, dk, dv):
    @pl.when(pl.program_id(0) == 0)
    def _():
        st_ref[...] = jnp.zeros_like(st_ref)

    def body(c, carry):
        r0 = pl.multiple_of(c * CHUNK, CHUNK)
        rows = pl.ds(r0, CHUNK)
        for b in range(q_ref.shape[0]):
            for h in range(RET_HEADS):
                qc = q_ref[b, rows, h * dk:(h + 1) * dk]
                kc = k_ref[b, rows, h * dk:(h + 1) * dk]
                vc = v_ref[b, rows, h * dv:(h + 1) * dv]
                state = st_ref[b, h]
                scores = lax.dot_general(qc, kc, (((1,), (1,)), ((), ())),
                                         preferred_element_type=F32) * intra_ref[h]
                o = (jnp.dot(scores.astype(BF16), vc, preferred_element_type=F32)
                     + jnp.dot(qc, state.astype(BF16), preferred_element_type=F32) * qd_ref[h])
                kdk = (kc.astype(F32) * kd_ref[h]).astype(BF16)
                st_ref[b, h] = state * cd_ref[h] + lax.dot_general(
                    kdk, vc, (((0,), (0,)), ((), ())), preferred_element_type=F32)
                mu = jnp.mean(o, axis=-1, keepdims=True)
                d = o - mu
                var = jnp.mean(d * d, axis=-1, keepdims=True)
                y = d * lax.rsqrt(var + EPS)
                o_ref[b, rows, h * dv:(h + 1) * dv] = (g_ref[b, rows, h * dv:(h + 1) * dv] * y).astype(o_ref.dtype)
        return carry

    lax.fori_loop(0, n_sub, body, 0)


def _retention(qkv, gate, tabs, B, S, lc):
    T = qkv.shape[0]
    D = qkv.shape[1] // 4
    dk = D // RET_HEADS
    dv = 2 * D // RET_HEADS
    H = RET_HEADS
    intra, qd, kd, cd = tabs
    qkv3 = qkv.reshape(B, S, 4 * D)
    gate3 = gate.reshape(B, S, 2 * D)
    out = pl.pallas_call(
        functools.partial(_ret_kernel, n_sub=lc // CHUNK, dk=dk, dv=dv),
        grid=(S // lc,),
        in_specs=[
            pl.BlockSpec((B, lc, D), lambda s: (0, s, 0)),
            pl.BlockSpec((B, lc, D), lambda s: (0, s, 1)),
            pl.BlockSpec((B, lc, 2 * D), lambda s: (0, s, 1)),
            pl.BlockSpec((B, lc, 2 * D), lambda s: (0, s, 0)),
            pl.BlockSpec((H, CHUNK, CHUNK), lambda s: (0, 0, 0)),
            pl.BlockSpec((H, CHUNK, 1), lambda s: (0, 0, 0)),
            pl.BlockSpec((H, CHUNK, 1), lambda s: (0, 0, 0)),
            pl.BlockSpec((H, 1, 1), lambda s: (0, 0, 0)),
        ],
        out_specs=pl.BlockSpec((B, lc, 2 * D), lambda s: (0, s, 0)),
        out_shape=jax.ShapeDtypeStruct((B, S, 2 * D), BF16),
        scratch_shapes=[pltpu.VMEM((B, H, dk, dv), F32)],
        compiler_params=_params("arbitrary"),
        name="retention",
    )(qkv3, qkv3, qkv3, gate3, intra, qd, kd, cd)
    return out.reshape(T, 2 * D)


def _retention_tables():
    h = jnp.arange(RET_HEADS, dtype=F32)
    log_g = jnp.log(1.0 - 2.0 ** (-5.0 - h))
    idx = jnp.arange(CHUNK, dtype=F32)
    intra = jnp.exp(log_g[:, None, None] * jnp.abs(idx[:, None] - idx[None, :]))
    qd = jnp.exp(log_g[:, None] * (idx[None, :] + 1.0))[:, :, None]
    kd = jnp.exp(log_g[:, None] * (CHUNK - 1.0 - idx[None, :]))[:, :, None]
    cd = jnp.exp(log_g * CHUNK)[:, None, None]
    return intra, qd, kd, cd


def _rotary_tables(S, d):
    inv = 1.0 / (ROPE_BASE ** (jnp.arange(0, d, 2, dtype=F32) / d))
    ang = jnp.arange(S).astype(F32)[:, None] * inv[None, :]
    return jnp.cos(ang), jnp.sin(ang)


def _proj_res_kernel(*refs, gated):
    if gated:
        o_in, og_ref, w_ref, h_ref, gt_ref, out_ref = refs
        x = (o_in[...] * _sigmoid(og_ref[...])).astype(BF16)
    else:
        x_ref, w_ref, h_ref, gt_ref, out_ref = refs
        x = x_ref[...]
    out_ref[...] = h_ref[...] + gt_ref[...] * jnp.dot(x, w_ref[...], preferred_element_type=F32)


def _proj_residual(x, w, w_layer, h, gate, S, tm, og=None, og_block=0):
    T, D = h.shape
    K = w.shape[1]
    per_b = S // tm
    gated = og is not None
    xs = (x, og) if gated else (x,)
    in_specs = [pl.BlockSpec((tm, K), lambda i: (i, 0))]
    if gated:
        in_specs.append(pl.BlockSpec((tm, K), lambda i: (i, og_block)))
    in_specs += [
        pl.BlockSpec((None, K, D), lambda i: (w_layer, 0, 0)),
        pl.BlockSpec((tm, D), lambda i: (i, 0)),
        pl.BlockSpec((None, 1, D), lambda i: (i // per_b, 0, 0)),
    ]
    return pl.pallas_call(
        functools.partial(_proj_res_kernel, gated=gated),
        grid=(T // tm,),
        in_specs=in_specs,
        out_specs=pl.BlockSpec((tm, D), lambda i: (i, 0)),
        out_shape=jax.ShapeDtypeStruct((T, D), F32),
        compiler_params=_params("parallel"),
        name="proj_residual",
    )(*xs, w, h, gate)


def _ffn_dense_kernel(h_ref, g_ref, sh_ref, sc_ref, gt_ref, wg_ref, wu_ref, wd_ref, o_ref):
    h = h_ref[...]
    a = _norm_mod(h, g_ref[...], sh_ref[...], sc_ref[...]).astype(BF16)
    gate = jnp.dot(a, wg_ref[...], preferred_element_type=F32)
    up = jnp.dot(a, wu_ref[...], preferred_element_type=F32)
    act = (_silu(gate) * up).astype(BF16)
    o_ref[...] = h + gt_ref[...] * jnp.dot(act, wd_ref[...], preferred_element_type=F32)


def _ffn_dense(h, g, sh, sc, gate, wg, wu, wd, layer, S, tm):
    T, D = h.shape
    F = wg.shape[2]
    per_b = S // tm
    once = pl.Buffered(1)
    return pl.pallas_call(
        _ffn_dense_kernel,
        grid=(T // tm,),
        in_specs=[
            pl.BlockSpec((tm, D), lambda i: (i, 0)),
            pl.BlockSpec((1, D), lambda i: (0, 0)),
            pl.BlockSpec((None, 1, D), lambda i: (i // per_b, 0, 0)),
            pl.BlockSpec((None, 1, D), lambda i: (i // per_b, 0, 0)),
            pl.BlockSpec((None, 1, D), lambda i: (i // per_b, 0, 0)),
            pl.BlockSpec((None, D, F), lambda i: (layer, 0, 0), pipeline_mode=once),
            pl.BlockSpec((None, D, F), lambda i: (layer, 0, 0), pipeline_mode=once),
            pl.BlockSpec((None, F, D), lambda i: (layer, 0, 0), pipeline_mode=once),
        ],
        out_specs=pl.BlockSpec((tm, D), lambda i: (i, 0)),
        out_shape=jax.ShapeDtypeStruct((T, D), F32),
        compiler_params=_params("parallel"),
        name="ffn_dense",
    )(h, g, sh, sc, gate, wg, wu, wd)


def _ffn_moe_kernel(ut_ref, ue_ref, ulo_ref, uhi_ref, ufirst_ref, uvalid_ref, x_ref, wg_ref, wu_ref,
                    wd_ref, o_ref, acc_scr):
    u = pl.program_id(0)
    f = pl.program_id(1)
    valid = uvalid_ref[u] > 0

    @pl.when(jnp.logical_and(jnp.logical_and(valid, ufirst_ref[u] > 0), f == 0))
    def _():
        acc_scr[...] = jnp.zeros_like(acc_scr)

    tm = x_ref.shape[0]
    half = tm // 2
    lo = ulo_ref[u]
    hi = uhi_ref[u]
    whole = jnp.logical_and(lo == 0, hi == tm)

    def swiglu(rows, r0, masked):
        x = x_ref[rows, :]
        gate = jnp.dot(x, wg_ref[...].astype(BF16), preferred_element_type=F32)
        up = jnp.dot(x, wu_ref[...].astype(BF16), preferred_element_type=F32)
        act = _silu(gate) * up
        if masked:
            row = r0 + lax.broadcasted_iota(jnp.int32, gate.shape, 0)
            act = jnp.where(jnp.logical_and(row >= lo, row < hi), act, 0.0)
        acc_scr[rows, :] += jnp.dot(act.astype(BF16), wd_ref[...].astype(BF16),
                                    preferred_element_type=F32)

    @pl.when(jnp.logical_and(valid, whole))
    def _():
        swiglu(slice(None), 0, False)

    for k in range(2):
        touches = jnp.logical_and(lo < (k + 1) * half, hi > k * half)

        @pl.when(jnp.logical_and(jnp.logical_and(valid, jnp.logical_not(whole)), touches))
        def _(k=k):
            swiglu(slice(k * half, (k + 1) * half), k * half, True)

    @pl.when(jnp.logical_and(valid, f == pl.num_programs(1) - 1))
    def _():
        o_ref[...] = acc_scr[...]


def _ffn_moe(units, xs, wg, wu, wd, layer, tm, tf):
    N, D = xs.shape
    F = wg.shape[3]
    nf = F // tf
    n_units = units[0].shape[0]

    def xrow(u, f, ut, ue, ulo, uhi, ufirst, uvalid):
        return (ut[u], 0)

    def wcol(u, f, ut, ue, ulo, uhi, ufirst, uvalid):
        return (layer, ue[u], 0, jnp.where(uvalid[u] > 0, f, nf - 1))

    def wrow(u, f, ut, ue, ulo, uhi, ufirst, uvalid):
        return (layer, ue[u], jnp.where(uvalid[u] > 0, f, nf - 1), 0)

    grid_spec = pltpu.PrefetchScalarGridSpec(
        num_scalar_prefetch=6,
        grid=(n_units, nf),
        in_specs=[
            pl.BlockSpec((tm, D), xrow),
            pl.BlockSpec((None, None, D, tf), wcol),
            pl.BlockSpec((None, None, D, tf), wcol),
            pl.BlockSpec((None, None, tf, D), wrow),
        ],
        out_specs=pl.BlockSpec((tm, D), xrow),
        scratch_shapes=[pltpu.VMEM((tm, D), F32)],
    )
    return pl.pallas_call(
        _ffn_moe_kernel,
        grid_spec=grid_spec,
        out_shape=jax.ShapeDtypeStruct((N, D), F32),
        compiler_params=_params("arbitrary", "arbitrary"),
        name="ffn_moe",
    )(*units, xs, wg, wu, wd)


def _combine_kernel(*refs, fused_final):
    h_ref, ya_ref, yb_ref, wt_ref, gt_ref = refs[:5]
    o_ref = refs[-1]
    wt = wt_ref[...]
    ff = wt[:, 0:1] * ya_ref[...] + wt[:, 1:2] * yb_ref[...]
    out = h_ref[...] + gt_ref[...] * ff
    if fused_final:
        g_ref, sh_ref, sc_ref = refs[5:8]
        out = _norm_mod(out, g_ref[...], sh_ref[...], sc_ref[...])
    o_ref[...] = out


def _moe_combine(h, ya, yb, wt8, gate, S, tm, final=None):
    T, D = h.shape
    per_b = S // tm
    row = pl.BlockSpec((tm, D), lambda i: (i, 0))
    per_batch = pl.BlockSpec((None, 1, D), lambda i: (i // per_b, 0, 0))
    in_specs = [row, row, row, pl.BlockSpec((tm, 8), lambda i: (i, 0)), per_batch]
    args = [h, ya, yb, wt8, gate]
    if final is not None:
        in_specs += [pl.BlockSpec((1, D), lambda i: (0, 0)), per_batch, per_batch]
        args += list(final)
    return pl.pallas_call(
        functools.partial(_combine_kernel, fused_final=final is not None),
        grid=(T // tm,),
        in_specs=in_specs,
        out_specs=row,
        out_shape=jax.ShapeDtypeStruct((T, D), F32),
        compiler_params=_params("parallel"),
        name="moe_combine",
    )(*args)


def _router_kernel(h_ref, g_ref, sh_ref, sc_ref, rw_ref, rb_ref, m_ref, idx_ref, wt_ref, *, n_e):
    a = _norm_mod(h_ref[...], g_ref[...], sh_ref[...], sc_ref[...]).astype(BF16)
    m_ref[...] = a
    logits = jnp.dot(a, rw_ref[...], preferred_element_type=F32) + rb_ref[...]
    lane = lax.broadcasted_iota(jnp.int32, logits.shape, 1)
    lane_f = lane.astype(F32)
    logits = jnp.where(lane < n_e, logits, NEG_BIG)
    m1 = jnp.max(logits, axis=-1, keepdims=True)
    i1 = jnp.min(jnp.where(logits == m1, lane_f, float(LANES)), axis=-1, keepdims=True)
    rest = jnp.where(lane_f == i1, NEG_BIG, logits)
    m2 = jnp.max(rest, axis=-1, keepdims=True)
    i2 = jnp.min(jnp.where(rest == m2, lane_f, float(LANES)), axis=-1, keepdims=True)
    e2 = jnp.exp(m2 - m1)
    w1 = 1.0 / (1.0 + e2)
    w2 = e2 * w1
    w = jnp.where(lane == 0, w1, jnp.where(lane == 1, w2, 0.0))
    ii = jnp.where(lane == 0, i1, jnp.where(lane == 1, i2, 0.0)).astype(jnp.int32)
    wt_ref[...] = w[:, :8]
    idx_ref[...] = ii[:, :8]


def _router(h, g, sh, sc, rw, rb, S, tm, n_e):
    T, D = h.shape
    per_b = S // tm
    return pl.pallas_call(
        functools.partial(_router_kernel, n_e=n_e),
        grid=(T // tm,),
        in_specs=[
            pl.BlockSpec((tm, D), lambda i: (i, 0)),
            pl.BlockSpec((1, D), lambda i: (0, 0)),
            pl.BlockSpec((None, 1, D), lambda i: (i // per_b, 0, 0)),
            pl.BlockSpec((None, 1, D), lambda i: (i // per_b, 0, 0)),
            pl.BlockSpec((D, LANES), lambda i: (0, 0)),
            pl.BlockSpec((1, LANES), lambda i: (0, 0)),
        ],
        out_specs=[
            pl.BlockSpec((tm, D), lambda i: (i, 0)),
            pl.BlockSpec((tm, 8), lambda i: (i, 0)),
            pl.BlockSpec((tm, 8), lambda i: (i, 0)),
        ],
        out_shape=[
            jax.ShapeDtypeStruct((T, D), BF16),
            jax.ShapeDtypeStruct((T, 8), jnp.int32),
            jax.ShapeDtypeStruct((T, 8), F32),
        ],
        compiler_params=_params("parallel"),
        name="router",
    )(h, g, sh, sc, rw, rb)


def _forget_kernel(h_ref, g_ref, sh_ref, sc_ref, wf_ref, bf_ref, o_ref, carry_ref, *, per_b):
    @pl.when(pl.program_id(0) % per_b == 0)
    def _():
        carry_ref[...] = jnp.zeros_like(carry_ref)

    a = _norm_mod(h_ref[...], g_ref[...], sh_ref[...], sc_ref[...]).astype(BF16)
    z = jnp.dot(a, wf_ref[...], preferred_element_type=F32) + bf_ref[...]
    lf = jnp.minimum(z, 0.0) - jnp.log(1.0 + jnp.exp(-jnp.abs(z)))
    tm = lf.shape[0]
    r = lax.broadcasted_iota(jnp.int32, (tm, tm), 0)
    c = lax.broadcasted_iota(jnp.int32, (tm, tm), 1)
    tri = jnp.where(c <= r, 1.0, 0.0).astype(F32)
    cs = jnp.dot(tri, lf, preferred_element_type=F32, precision=lax.Precision.HIGHEST)
    out = cs + carry_ref[...]
    carry_ref[...] = out[tm - 1:tm, :]
    out2 = out * LOG2E
    hi = out2.astype(BF16)
    r1 = out2 - hi.astype(F32)
    mid = r1.astype(BF16)
    lo = (r1 - mid.astype(F32)).astype(BF16)
    o_ref[:, 0:LANES] = hi
    o_ref[:, LANES:2 * LANES] = mid
    o_ref[:, 2 * LANES:3 * LANES] = lo


def _forget_cumsum(h, g, sh, sc, wf, bf, S, tm):
    T, D = h.shape
    per_b = S // tm
    return pl.pallas_call(
        functools.partial(_forget_kernel, per_b=per_b),
        grid=(T // tm,),
        in_specs=[
            pl.BlockSpec((tm, D), lambda i: (i, 0)),
            pl.BlockSpec((1, D), lambda i: (0, 0)),
            pl.BlockSpec((None, 1, D), lambda i: (i // per_b, 0, 0)),
            pl.BlockSpec((None, 1, D), lambda i: (i // per_b, 0, 0)),
            pl.BlockSpec((D, LANES), lambda i: (0, 0)),
            pl.BlockSpec((1, LANES), lambda i: (0, 0)),
        ],
        out_specs=pl.BlockSpec((tm, 3 * LANES), lambda i: (i, 0)),
        out_shape=jax.ShapeDtypeStruct((T, 3 * LANES), BF16),
        scratch_shapes=[pltpu.VMEM((1, LANES), F32)],
        compiler_params=_params("arbitrary"),
        name="forget_cumsum",
    )(h, g, sh, sc, wf, bf)


def _fox_prep_kernel(kp_ref, vp_ref, f_ref, pm_ref, ka_ref, vt_ref, kn_ref, *, dh, tk):
    c = pl.program_id(2)
    n_sub = kp_ref.shape[0] // tk
    lane = lax.broadcasted_iota(jnp.int32, (tk, LANES), 1)
    lane8 = lax.broadcasted_iota(jnp.int32, (8, LANES), 1)
    row8 = lax.broadcasted_iota(jnp.int32, (8, LANES), 0)
    row16 = lax.broadcasted_iota(jnp.int32, (DV_ROWS - dh, tk), 0)
    tail = jnp.where(row16 == 0, 1.0, 0.0).astype(BF16)

    @pl.when(c == 0)
    def _():
        kn_ref[...] = jnp.zeros_like(kn_ref)

    for j in range(n_sub):
        rows = slice(j * tk, (j + 1) * tk)
        kp = kp_ref[rows, :]
        ex = jnp.dot(f_ref[rows, :], pm_ref[...], preferred_element_type=F32)
        ex = jnp.where((lane % dh) < 3, 1.0, ex).astype(BF16)
        ka_ref[0, rows, :] = jnp.where(lane < dh, kp, ex)
        ka_ref[1, rows, :] = jnp.where(lane < dh, ex, kp)
        vt = vp_ref[rows, :].astype(F32).T
        vt_ref[0, 0:dh, rows] = vt[0:dh].astype(BF16)
        vt_ref[1, 0:dh, rows] = vt[dh:2 * dh].astype(BF16)
        vt_ref[0, dh:DV_ROWS, rows] = tail
        vt_ref[1, dh:DV_ROWS, rows] = tail
        k2 = kp.astype(F32)
        k2 = k2 * k2
        na = jnp.max(jnp.sum(jnp.where(lane < dh, k2, 0.0), axis=1, keepdims=True), axis=0, keepdims=True)
        nb = jnp.max(jnp.sum(jnp.where(lane < dh, 0.0, k2), axis=1, keepdims=True), axis=0, keepdims=True)
        upd = jnp.where(row8 == 0, na, jnp.where(row8 == 1, nb, 0.0))
        kn_ref[...] = jnp.where(lane8 == c * n_sub + j, upd, kn_ref[...])


def _fox_prep(kv, fparts, pm, B, S, D, dh, tk, tp):
    H = D // dh
    nb = S // tp
    assert S // tk <= LANES and tp % tk == 0
    return pl.pallas_call(
        functools.partial(_fox_prep_kernel, dh=dh, tk=tk),
        grid=(B, H // 2, nb),
        in_specs=[
            pl.BlockSpec((tp, LANES), lambda b, p, c: (b * nb + c, p)),
            pl.BlockSpec((tp, LANES), lambda b, p, c: (b * nb + c, D // LANES + p)),
            pl.BlockSpec((tp, 3 * LANES), lambda b, p, c: (b * nb + c, 0)),
            pl.BlockSpec((None, 3 * LANES, LANES), lambda b, p, c: (p, 0, 0)),
        ],
        out_specs=[
            pl.BlockSpec((None, 2, tp, LANES), lambda b, p, c: (b, p, c, 0)),
            pl.BlockSpec((None, 2, DV_ROWS, tp), lambda b, p, c: (b, p, 0, c)),
            pl.BlockSpec((None, None, 8, LANES), lambda b, p, c: (b, p, 0, 0)),
        ],
        out_shape=[
            jax.ShapeDtypeStruct((B, H, S, LANES), BF16),
            jax.ShapeDtypeStruct((B, H, DV_ROWS, S), BF16),
            jax.ShapeDtypeStruct((B, H // 2, 8, LANES), F32),
        ],
        compiler_params=_params("parallel", "parallel", "arbitrary"),
        name="fox_prep",
    )(kv, kv, fparts, pm)


def _fox_kernel(q_ref, gq_ref, gb_ref, kn_ref, ka_ref, vt_ref, o_ref, sa0, sa1, sb0, sb1, *,
                tq, tk, dh):
    i = pl.program_id(2)
    qT = q_ref[...].T
    g16 = gq_ref[...]
    row8 = lax.broadcasted_iota(jnp.int32, (8, tq), 0)
    xa = jnp.where(row8 < 3, g16[0:8], jnp.where(row8 < 6, 1.0, 0.0))
    xb = jnp.where(row8 < 3, g16[8:16], jnp.where(row8 < 6, 1.0, 0.0))
    zpad = jnp.zeros((LANES - dh - 8, tq), F32)
    qa = jnp.concatenate([qT[0:dh], xa, zpad], axis=0).astype(BF16)
    qb = jnp.concatenate([xb, zpad, qT[dh:2 * dh]], axis=0).astype(BF16)

    def scores(h, q, blk):
        k0 = pl.multiple_of(blk * tk, tk)
        return jnp.dot(ka_ref[h, pl.ds(k0, tk), :], q, preferred_element_type=F32)

    def update(s_ref, h, blk, m, acc):
        k0 = pl.multiple_of(blk * tk, tk)
        s = s_ref[...]
        m_new = jnp.maximum(m, jnp.max(s, axis=0, keepdims=True))
        alpha = jnp.exp2(m - m_new)
        p = jnp.exp2(s - m_new).astype(BF16)
        pv = jnp.dot(vt_ref[h, :, pl.ds(k0, tk)], p, preferred_element_type=F32)
        return m_new, alpha * acc + pv

    q2 = qT * qT
    qna = jnp.max(jnp.sum(q2[0:dh], axis=0, keepdims=True), axis=1, keepdims=True)
    qnb = jnp.max(jnp.sum(q2[dh:2 * dh], axis=0, keepdims=True), axis=1, keepdims=True)
    kn = kn_ref[...]
    gb = gb_ref[...]
    lane = lax.broadcasted_iota(jnp.int32, (1, LANES), 1)
    at_i = lane == i

    def pick(row):
        return jnp.sum(jnp.where(at_i, row, 0.0), axis=1, keepdims=True)

    def needed(qn, knr, glast, gfirst):
        bound = NORM_SLACK * jnp.sqrt(qn) * (jnp.sqrt(knr) + jnp.sqrt(pick(knr))) + pick(gfirst) - glast
        return bound > -SKIP_LOG2

    need = jnp.logical_or(needed(qna, kn[0:1], gb[0:1], gb[2:3]), needed(qnb, kn[1:2], gb[1:2], gb[3:4]))
    lane_f = lane.astype(F32)
    i_f = i.astype(F32)
    cand = jnp.where(jnp.logical_and(need, lane < i), lane_f, i_f)
    jmin = jnp.min(cand).astype(jnp.int32)

    key = lax.broadcasted_iota(jnp.int32, (tk, tq), 0)
    qry = lax.broadcasted_iota(jnp.int32, (tk, tq), 1)
    sa0[...] = jnp.where(key <= qry, scores(0, qa, i), NEG_BIG)
    sb0[...] = jnp.where(key <= qry, scores(1, qb, i), NEG_BIG)
    ma0 = jnp.max(sa0[...], axis=0, keepdims=True)
    mb0 = jnp.max(sb0[...], axis=0, keepdims=True)

    prev = jnp.maximum(i - 1, 0)
    sa1[...] = jnp.where(i > 0, scores(0, qa, prev), NEG_BIG)
    sb1[...] = jnp.where(i > 0, scores(1, qb, prev), NEG_BIG)

    def first(s_ref, h, m0):
        k0 = pl.multiple_of(i * tk, tk)
        p = jnp.exp2(s_ref[...] - m0).astype(BF16)
        return jnp.dot(vt_ref[h, :, pl.ds(k0, tk)], p, preferred_element_type=F32)

    acca0 = first(sa0, 0, ma0)
    accb0 = first(sb0, 1, mb0)
    ma1, acca1 = update(sa1, 0, prev, ma0, acca0)
    mb1, accb1 = update(sb1, 1, prev, mb0, accb0)

    n_off = jnp.maximum(i - 1 - jmin, 0)
    n_pairs = jnp.maximum((n_off + 1) // 2, 1)

    def blk_at(pos):
        return jnp.minimum(jmin + pos, i)

    def masked_scores(h, q, pos):
        return jnp.where(pos < n_off, scores(h, q, blk_at(pos)), NEG_BIG)

    sa0[...] = masked_scores(0, qa, 0)
    sb0[...] = masked_scores(1, qb, 0)

    def pair(u, carry):
        ma, acca, mb, accb = carry
        b0 = jmin + 2 * u
        sa1[...] = scores(0, qa, b0 + 1)
        sb1[...] = scores(1, qb, b0 + 1)
        ma, acca = update(sa0, 0, b0, ma, acca)
        mb, accb = update(sb0, 1, b0, mb, accb)
        sa0[...] = scores(0, qa, b0 + 2)
        sb0[...] = scores(1, qb, b0 + 2)
        ma, acca = update(sa1, 0, b0 + 1, ma, acca)
        mb, accb = update(sb1, 1, b0 + 1, mb, accb)
        return ma, acca, mb, accb

    carry = lax.fori_loop(0, n_pairs - 1, pair, (ma1, acca1, mb1, accb1))
    u = n_pairs - 1

    def tail_two(carry):
        ma, acca, mb, accb = carry
        sa1[...] = scores(0, qa, blk_at(2 * u + 1))
        sb1[...] = scores(1, qb, blk_at(2 * u + 1))
        ma, acca = update(sa0, 0, blk_at(2 * u), ma, acca)
        mb, accb = update(sb0, 1, blk_at(2 * u), mb, accb)
        ma, acca = update(sa1, 0, blk_at(2 * u + 1), ma, acca)
        mb, accb = update(sb1, 1, blk_at(2 * u + 1), mb, accb)
        return ma, acca, mb, accb

    def tail_one(carry):
        ma, acca, mb, accb = carry
        ma, acca = update(sa0, 0, blk_at(2 * u), ma, acca)
        mb, accb = update(sb0, 1, blk_at(2 * u), mb, accb)
        return ma, acca, mb, accb

    ma, acca, mb, accb = lax.cond(2 * u + 1 < n_off, tail_two, tail_one, carry)
    oa = acca[:dh, :] * (1.0 / acca[dh:dh + 1, :])
    ob = accb[:dh, :] * (1.0 / accb[dh:dh + 1, :])
    o_ref[...] = jnp.concatenate([oa, ob], axis=0).T


def _fox_attention(qg, gq, gb, kn, ka, vt, B, S, D, dh, tq):
    T = B * S
    H = D // dh
    nq = S // tq
    return pl.pallas_call(
        functools.partial(_fox_kernel, tq=tq, tk=tq, dh=dh),
        grid=(B, H // 2, nq),
        in_specs=[
            pl.BlockSpec((tq, LANES), lambda b, p, i: (b * nq + i, p)),
            pl.BlockSpec((None, None, 16, tq), lambda b, p, i: (b, p, 0, i)),
            pl.BlockSpec((None, None, 8, LANES), lambda b, p, i: (b, p, 0, 0)),
            pl.BlockSpec((None, None, 8, LANES), lambda b, p, i: (b, p, 0, 0)),
            pl.BlockSpec((None, 2, S, LANES), lambda b, p, i: (b, p, 0, 0)),
            pl.BlockSpec((None, 2, DV_ROWS, S), lambda b, p, i: (b, p, 0, 0)),
        ],
        out_specs=pl.BlockSpec((tq, LANES), lambda b, p, i: (b * nq + i, p)),
        out_shape=jax.ShapeDtypeStruct((T, D), F32),
        scratch_shapes=[pltpu.VMEM((tq, tq), F32) for _ in range(4)],
        compiler_params=_params("parallel", "parallel", "arbitrary"),
        name="fox_attention",
    )(qg, gq, gb, kn, ka, vt)


def _fox_gate_tables(fparts, B, S, H, tk):
    f3 = fparts.reshape(B, S, 3, LANES)[:, :, :, :H].astype(F32)
    gq = f3.reshape(B, S, 3, H // 2, 2).transpose(0, 3, 4, 2, 1)
    gq = jnp.pad(gq, ((0, 0), (0, 0), (0, 0), (0, 5), (0, 0))).reshape(B, H // 2, 16, S)
    G = jnp.sum(f3, axis=2)
    nb = S // tk
    Gb = G.reshape(B, nb, tk, H // 2, 2)
    last = Gb[:, :, tk - 1].transpose(0, 2, 3, 1)
    first = Gb[:, :, 0].transpose(0, 2, 3, 1)
    gb = jnp.concatenate([last, first, jnp.zeros_like(last), jnp.zeros_like(last)], axis=2)
    gb = jnp.pad(gb, ((0, 0), (0, 0), (0, 0), (0, LANES - nb)))
    return gq, gb


def _fox_placement(H, dh):
    p = jnp.arange(H // 2)[:, None, None]
    r = jnp.arange(3 * LANES)[None, :, None]
    l = jnp.arange(LANES)[None, None, :]
    j = r // LANES
    head = r % LANES
    hit_b = jnp.logical_and(head == 2 * p + 1, l == 3 + j)
    hit_a = jnp.logical_and(head == 2 * p, l == dh + 3 + j)
    return jnp.where(jnp.logical_or(hit_a, hit_b), -1.0, 0.0).astype(BF16)


def _final_kernel(h_ref, g_ref, sh_ref, sc_ref, o_ref):
    o_ref[...] = _norm_mod(h_ref[...], g_ref[...], sh_ref[...], sc_ref[...])


def _final_norm(h, g, sh, sc, S, tm):
    T, D = h.shape
    per_b = S // tm
    return pl.pallas_call(
        _final_kernel,
        grid=(T // tm,),
        in_specs=[
            pl.BlockSpec((tm, D), lambda i: (i, 0)),
            pl.BlockSpec((1, D), lambda i: (0, 0)),
            pl.BlockSpec((None, 1, D), lambda i: (i // per_b, 0, 0)),
            pl.BlockSpec((None, 1, D), lambda i: (i // per_b, 0, 0)),
        ],
        out_specs=pl.BlockSpec((tm, D), lambda i: (i, 0)),
        out_shape=jax.ShapeDtypeStruct((T, D), F32),
        compiler_params=_params("parallel"),
        name="final_norm",
    )(h, g, sh, sc)


def _dispatch_tables(top_idx, n_e, tm):
    T, K = top_idx.shape
    n = T * K
    n_tiles = n // tm
    n_units = n_tiles + n_e
    e_flat = top_idx.reshape(n)
    order = jnp.argsort(e_flat, stable=True).astype(jnp.int32)
    pos = jnp.argsort(order).astype(jnp.int32)
    ids = jnp.arange(n_e, dtype=jnp.int32)
    counts = jnp.sum(e_flat[:, None] == ids[None, :], axis=0).astype(jnp.int32)
    ends = jnp.cumsum(counts)
    starts = ends - counts
    t0 = jnp.arange(n_tiles, dtype=jnp.int32)[:, None] * tm
    present = jnp.logical_and(starts[None, :] < t0 + tm, ends[None, :] > t0)
    present = jnp.logical_and(present, counts[None, :] > 0)
    flat = jnp.arange(n_tiles * n_e, dtype=jnp.int32)
    keys = jnp.sort(jnp.where(present.reshape(-1), flat, n_tiles * n_e))[:n_units]
    valid = keys < n_tiles * n_e
    n_valid = jnp.sum(valid.astype(jnp.int32))
    keys = jnp.where(valid, keys, keys[jnp.maximum(n_valid - 1, 0)])
    ut = keys // n_e
    ue = keys % n_e
    ulo = jnp.clip(starts[ue] - ut * tm, 0, tm)
    uhi = jnp.clip(ends[ue] - ut * tm, 0, tm)
    prev_t = jnp.concatenate([jnp.full((1,), -1, jnp.int32), ut[:-1]])
    ufirst = (ut != prev_t).astype(jnp.int32)
    units = tuple(a.astype(jnp.int32) for a in (ut, ue, ulo, uhi, ufirst, valid))
    return order, pos.reshape(T, K), units


def _tile(S, want):
    t = min(S, want)
    assert S % t == 0
    return t


def _mxu_tile(F, cap):
    fits = [t for t in range(MXU_DIM, min(F, cap) + 1, MXU_DIM) if F % t == 0]
    return max(fits) if fits else F


def kernel(x, c, ada_w, ada_b, norm_g, ret_w_in, ret_w_o, kv_ada_w, kv_ada_b, kv_norm_g, fox_w_kv, fox_w_f, fox_b_f, fox_w_qg, fox_w_o, ffn_w_gate, ffn_w_up, ffn_w_down, router_w, router_b, moe_w_gate, moe_w_up, moe_w_down, final_ada_w, final_ada_b, final_norm_g):
    B, S, D = x.shape
    T = B * S
    depth = ada_w.shape[0]
    n_a = ret_w_in.shape[0]
    n_e = router_w.shape[-1]
    dh = D // FOX_HEADS
    dk = D // RET_HEADS
    assert dk == 2 * LANES and S % CHUNK == 0 and n_e <= 8

    tm = _tile(S, 512)
    tm_moe = _tile(S, 1024)
    tq = _tile(S, 512)
    lc = _tile(S, 256)
    f_moe = moe_w_gate.shape[-1]
    tf_moe = _mxu_tile(f_moe, 1024)

    c_pad = jnp.zeros((8, D), F32).at[:B].set(c)
    ada = _ada(c_pad, ada_w, ada_b[:, None, :])[:, :B]
    extra_w = jnp.stack([kv_ada_w, final_ada_w])
    extra_b = jnp.stack([kv_ada_b, final_ada_b])[:, None, :]
    extra = _ada(c_pad, extra_w, extra_b)[:, :B]
    vec = lambda a: a[:, None, :]

    h = x.reshape(T, D)
    cos, sin = _rotary_tables(S, dk)
    ret_tabs = _retention_tables()
    shared = None
    w_in_b, w_ro_b = ret_w_in.astype(BF16), ret_w_o.astype(BF16)
    w_kv_b, w_qg_b, w_fo_b = fox_w_kv.astype(BF16)[None], fox_w_qg.astype(BF16), fox_w_o.astype(BF16)
    w_fg_b, w_fu_b, w_fd_b = ffn_w_gate.astype(BF16), ffn_w_up.astype(BF16), ffn_w_down.astype(BF16)

    f_sh, f_sc = [vec(a) for a in jnp.split(extra[1], 2, axis=-1)]
    final = (final_norm_g[None, :], f_sh, f_sc)

    for l in range(depth):
        sh1, sc1, g1, sh2, sc2, g2 = [vec(a) for a in jnp.split(ada[l], 6, axis=-1)]
        ng1 = norm_g[l, 0][None, :]
        ng2 = norm_g[l, 1][None, :]
        if l == n_a:
            kv_sh, kv_sc = [vec(a) for a in jnp.split(extra[0], 2, axis=-1)]
            kvg = kv_norm_g[None, :]
            kv = _norm_mod_matmul(h, kvg, kv_sh, kv_sc, w_kv_b, S,
                                  modes=("plain",) * 4, scales=(1.0,) * 4, cw=D // 2,
                                  out_dtype=BF16, tm=tm)
            wf = jnp.zeros((D, LANES), F32).at[:, :FOX_HEADS].set(fox_w_f).astype(BF16)
            bf = jnp.zeros((1, LANES), F32).at[0, :FOX_HEADS].set(fox_b_f)
            fparts = _forget_cumsum(h, kvg, kv_sh, kv_sc, wf, bf, S, tm)
            ka, vt, kn = _fox_prep(kv, fparts, _fox_placement(FOX_HEADS, dh), B, S, D, dh, tq,
                                   _tile(S, 4 * tq))
            gq, gb = _fox_gate_tables(fparts, B, S, FOX_HEADS, tq)
            shared = (gq, gb, kn, ka, vt)
        if l < n_a:
            qkv = _norm_mod_matmul(
                h, ng1, sh1, sc1, w_in_b, S,
                modes=("rot",) * 4 + ("plain",) * 4,
                scales=(1.0,) * 2 + (dk ** -0.5,) * 2 + (1.0,) * 4,
                cw=D // 2, out_dtype=BF16, tm=tm, rot=(cos, sin), w_layer=l, w_col=0)
            gate = _norm_mod_matmul(h, ng1, sh1, sc1, w_in_b, S,
                                    modes=("silu",) * 4, scales=(1.0,) * 4, cw=D // 2,
                                    out_dtype=F32, tm=tm, w_layer=l, w_col=2)
            y = _retention(qkv, gate, ret_tabs, B, S, lc)
            h = _proj_residual(y, w_ro_b, l, h, g1, S, tm)
        else:
            j = l - n_a
            qg = _norm_mod_matmul(h, ng1, sh1, sc1, w_qg_b, S,
                                  modes=("plain",) * 4, scales=(dh ** -0.5 * LOG2E,) * 2 + (1.0,) * 2,
                                  cw=D // 2, out_dtype=F32, tm=tm, w_layer=j)
            o = _fox_attention(qg, *shared, B, S, D, dh, tq)
            h = _proj_residual(o, w_fo_b, j, h, g1, S, tm, og=qg, og_block=1)
        i = l // 2
        if l % 2 == 0:
            h = _ffn_dense(h, ng2, sh2, sc2, g2, w_fg_b, w_fu_b, w_fd_b, i, S, tm)
        else:
            rw = jnp.zeros((D, LANES), F32).at[:, :n_e].set(router_w[i]).astype(BF16)
            rb = jnp.zeros((1, LANES), F32).at[0, :n_e].set(router_b[i])
            m, idx8, wt8 = _router(h, ng2, sh2, sc2, rw, rb, S, tm, n_e)
            order, pos, units = _dispatch_tables(idx8[:, :TOP_K], n_e, tm_moe)
            xs = m.at[order // TOP_K].get(mode="promise_in_bounds")
            ys = _ffn_moe(units, xs, moe_w_gate, moe_w_up, moe_w_down, i, tm_moe, tf_moe)
            ya = ys.at[pos[:, 0]].get(mode="promise_in_bounds")
            yb = ys.at[pos[:, 1]].get(mode="promise_in_bounds")
            h = _moe_combine(h, ya, yb, wt8, g2, S, tm, final=final if l == depth - 1 else None)

    if (depth - 1) % 2 == 0:
        h = _final_norm(h, *final, S, tm)
    return h.reshape(B, S, D)
```
